```python
import functools
import jax
import jax.numpy as jnp
from jax import lax
import numpy as np

D_MODEL = 4096
BATCH = 4
SEQ = 2048
DEPTH = 1
DEC_BATCH = 128
DEC_SEQ = 8
PAST_LEN = 16384
PAGE_SIZE = 128

RW_HEADS = 32
RW_HEAD_DIM = 64
RW_WIDTH = RW_HEADS * RW_HEAD_DIM
DECAY_RANK = 64
ICLR_RANK = 64
GATE_RANK = 256
RW_COLS = 3 * RW_WIDTH + DECAY_RANK + ICLR_RANK + GATE_RANK
GN_EPS = 64e-5
MLA_HEADS = 16
QK_NOPE = 128
QK_ROPE = 64
V_DIM = 128
Q_RANK = 1024
KV_RANK = 512
MLA_WIDTH = MLA_HEADS * V_DIM
ROPE_THETA = 10000.0
Q_BLOCK = 128
ATTN_SCALE = (QK_NOPE + QK_ROPE) ** -0.5
N_BRANCH = 2
BRANCH_WIDTH = RW_WIDTH
IN_COLS = RW_COLS + Q_RANK + KV_RANK + QK_ROPE + N_BRANCH * D_MODEL
N_EXPERTS = 256
TOP_K = 8
N_GROUPS = 8
TOPK_GROUPS = 4
D_EXPERT = 512
ROUTED_SCALE = 2.5
EXPERT_BLOCK = 128
DN_ALPHA = (2.0 * DEPTH) ** 0.25
DN_BETA = (8.0 * DEPTH) ** -0.25
LN_EPS = 1e-5
RMS_EPS = 1e-6

kernel_name = 'hybrid_rwkv7_mla_moe_deepnorm_step'


def layer_norm(x, g, b):
    xf = x.astype(jnp.float32)
    mu = jnp.mean(xf, axis=-1, keepdims=True)
    var = jnp.mean(jnp.square(xf - mu), axis=-1, keepdims=True)
    return ((xf - mu) * lax.rsqrt(var + LN_EPS) * g + b).astype(x.dtype)


def rms_norm(x, g):
    xf = x.astype(jnp.float32)
    return (xf * lax.rsqrt(jnp.mean(jnp.square(xf), axis=-1, keepdims=True) + RMS_EPS) * g).astype(x.dtype)


def rope_tables(pos):
    half = QK_ROPE // 2
    inv = ROPE_THETA ** (-jnp.arange(half, dtype=jnp.float32) / half)
    ang = pos.astype(jnp.float32)[:, None] * inv[None, :]
    return jnp.cos(ang), jnp.sin(ang)


def apply_rope(x, cos, sin):
    xf = x.astype(jnp.float32)
    x1, x2 = jnp.split(xf, 2, axis=-1)
    return jnp.concatenate([x1 * cos - x2 * sin, x2 * cos + x1 * sin], axis=-1).astype(x.dtype)


def rwkv7_branch(p, shift_prev, wkv_prev, mu, w0, w_dec, a0, w_icl, w_g, k_k, k_a, r_k, gn_g, gn_b):
    B, S, _ = p.shape
    H, N, W = RW_HEADS, RW_HEAD_DIM, RW_WIDTH
    f32 = jnp.float32
    p_prev = jnp.concatenate([shift_prev[:, None, :].astype(p.dtype), p[:, :-1]], axis=1)
    ps = p + (p_prev - p) * mu
    r, k, v, xw, xa, xg = jnp.split(ps, [W, 2 * W, 3 * W, 3 * W + DECAY_RANK, 3 * W + DECAY_RANK + ICLR_RANK], axis=-1)
    w_log = -jax.nn.softplus(-(w0 + jnp.tanh(xw) @ w_dec).astype(f32)) - 0.5
    decay = jnp.exp(-jnp.exp(w_log))
    a = jax.nn.sigmoid((a0 + xa @ w_icl).astype(f32))
    g = jax.nn.sigmoid(xg) @ w_g
    heads = lambda t: t.astype(f32).reshape(B, S, H, N)
    r_h, k_h, v_h, a_h, w_h = heads(r), heads(k), heads(v), heads(a), heads(decay)
    kk = heads(k * k_k)
    kk = kk / jnp.maximum(jnp.sqrt(jnp.sum(jnp.square(kk), axis=-1, keepdims=True)), 1e-12)
    k_h = k_h * (1.0 + (a_h - 1.0) * k_a.reshape(H, N).astype(f32))

    def step(state, inp):
        r_t, w_t, k_t, v_t, kk_t, b_t = inp
        sa = jnp.einsum('bhvk,bhk->bhv', state, -kk_t)
        state = (state * w_t[:, :, None, :] + sa[..., None] * b_t[:, :, None, :]
                 + v_t[..., None] * k_t[:, :, None, :])
        return state, jnp.einsum('bhvk,bhk->bhv', state, r_t)

    seq_first = lambda t: jnp.swapaxes(t, 0, 1)
    xs = (seq_first(r_h), seq_first(w_h), seq_first(k_h), seq_first(v_h), seq_first(kk), seq_first(kk * a_h))
    wkv_new, y = lax.scan(step, wkv_prev.astype(f32), xs)
    y = seq_first(y)
    y_mu = jnp.mean(y, axis=-1, keepdims=True)
    y_var = jnp.mean(jnp.square(y - y_mu), axis=-1, keepdims=True)
    yn = (y - y_mu) * lax.rsqrt(y_var + GN_EPS) * gn_g.reshape(H, N) + gn_b.reshape(H, N)
    bonus = jnp.sum(r_h * k_h * r_k.astype(f32), axis=-1, keepdims=True) * v_h
    out = (yn + bonus).reshape(B, S, W).astype(p.dtype) * g
    return out, p[:, -1], wkv_new


def mla_project(p_q, p_kv, pos, q_norm_g, w_q_b, kv_norm_g, w_uk):
    q = jnp.einsum('bsr,rhd->bshd', rms_norm(p_q, q_norm_g), w_q_b)
    q_nope, q_pe = q[..., :QK_NOPE], q[..., QK_NOPE:]
    c_kv = rms_norm(p_kv[..., :KV_RANK], kv_norm_g)
    cos, sin = rope_tables(pos)
    q_pe = apply_rope(q_pe, cos[:, None, :], sin[:, None, :])
    k_pe = apply_rope(p_kv[..., KV_RANK:], cos, sin)
    q_lat = jnp.einsum('bshd,chd->bshc', q_nope, w_uk)
    return q_lat, q_pe, c_kv, k_pe


def latent_scores(q_lat, q_pe, c_kv, k_pe):
    s = jnp.einsum('bqhc,bkc->bhqk', q_lat, c_kv) + jnp.einsum('bqhr,bkr->bhqk', q_pe, k_pe)
    return s.astype(jnp.float32) * ATTN_SCALE


def mla_attend_prompt(q_lat, q_pe, c_kv, k_pe):
    B, S, H, C = q_lat.shape
    n_blk = S // Q_BLOCK
    to_blocks = lambda t: jnp.swapaxes(t.reshape((B, n_blk, Q_BLOCK) + t.shape[2:]), 0, 1)
    key_pos = jnp.arange(S)

    def block(args):
        i, ql, qp = args
        s = latent_scores(ql, qp, c_kv, k_pe)
        q_pos = i * Q_BLOCK + jnp.arange(Q_BLOCK)
        s = jnp.where(key_pos[None, :] <= q_pos[:, None], s, -jnp.inf)
        pr = jax.nn.softmax(s, axis=-1).astype(c_kv.dtype)
        return jnp.einsum('bhqk,bkc->bqhc', pr, c_kv)

    o = lax.map(block, (jnp.arange(n_blk), to_blocks(q_lat), to_blocks(q_pe)))
    return jnp.swapaxes(o, 0, 1).reshape(B, S, H, C)


def mla_attend_sample(q_lat, q_pe, c_kv, k_pe, layer, cache_kv, cache_kpe, page_table):
    T = q_lat.shape[1]
    s = latent_scores(q_lat, q_pe, c_kv, k_pe)
    s = jnp.where(jnp.arange(T)[None, :] <= jnp.arange(T)[:, None], s, -jnp.inf)
    m0 = jnp.max(s, axis=-1)
    p0 = jnp.exp(s - m0[..., None])
    carry0 = (m0, jnp.sum(p0, axis=-1), jnp.einsum('bhqk,bkc->bhqc', p0, c_kv.astype(jnp.float32)))

    def page_step(carry, pages):
        m, l, acc = carry
        ck = cache_kv[layer, pages]
        kp = cache_kpe[layer, pages]
        s = latent_scores(q_lat, q_pe, ck, kp)
        m_new = jnp.maximum(m, jnp.max(s, axis=-1))
        corr = jnp.exp(m - m_new)
        p = jnp.exp(s - m_new[..., None])
        l = l * corr + jnp.sum(p, axis=-1)
        acc = acc * corr[..., None] + jnp.einsum('bhqk,bkc->bhqc', p, ck.astype(jnp.float32))
        return (m_new, l, acc), None

    (m, l, acc), _ = lax.scan(page_step, carry0, page_table.T)
    return jnp.swapaxes(acc / l[..., None], 1, 2).astype(q_lat.dtype)


def route(x2d, w_router, router_bias):
    T = x2d.shape[0]
    s = jax.nn.sigmoid((x2d @ w_router).astype(jnp.float32))
    sc = s + router_bias.astype(jnp.float32)
    grp = jnp.sum(lax.top_k(sc.reshape(T, N_GROUPS, N_EXPERTS // N_GROUPS), 2)[0], axis=-1)
    _, gidx = lax.top_k(grp, TOPK_GROUPS)
    gmask = jnp.sum(jax.nn.one_hot(gidx, N_GROUPS), axis=1) > 0
    emask = jnp.repeat(gmask, N_EXPERTS // N_GROUPS, axis=1)
    _, idx = lax.top_k(jnp.where(emask, sc, -jnp.inf), TOP_K)
    wts = jnp.take_along_axis(s, idx, axis=1)
    wts = wts / jnp.sum(wts, axis=-1, keepdims=True) * ROUTED_SCALE
    return idx, wts


def routed_experts(x2d, idx, wts, layer, w_exp_up, w_exp_down):
    T, D = x2d.shape
    M = T * TOP_K
    n_blocks = -(-M // EXPERT_BLOCK) + N_EXPERTS
    flat_e = idx.reshape(M)
    order = jnp.argsort(flat_e)
    e_sorted = flat_e[order]
    tok_sorted = (order // TOP_K).astype(jnp.int32)
    w_sorted = wts.reshape(M)[order]
    counts = jnp.bincount(flat_e, length=N_EXPERTS)
    blocks_per_e = (counts + EXPERT_BLOCK - 1) // EXPERT_BLOCK
    blk_end = jnp.cumsum(blocks_per_e)
    blk_start = blk_end - blocks_per_e
    row_start = jnp.cumsum(counts) - counts
    dest = blk_start[e_sorted] * EXPERT_BLOCK + jnp.arange(M) - row_start[e_sorted]
    slot_tok = jnp.full((n_blocks * EXPERT_BLOCK,), T, jnp.int32).at[dest].set(tok_sorted)
    slot_w = jnp.zeros((n_blocks * EXPERT_BLOCK,), wts.dtype).at[dest].set(w_sorted)
    blk_expert = jnp.minimum(jnp.searchsorted(blk_end, jnp.arange(n_blocks), side='right'), N_EXPERTS - 1)
    x_pad = jnp.concatenate([x2d, jnp.zeros((1, D), x2d.dtype)], axis=0)

    def body(y, blk):
        toks, gw, e = blk
        h = x_pad[toks] @ w_exp_up[layer, e]
        out = (jax.nn.silu(h[:, :D_EXPERT]) * h[:, D_EXPERT:]) @ w_exp_down[layer, e]
        return y.at[toks].add(out * gw[:, None].astype(out.dtype)), None

    y, _ = lax.scan(body, jnp.zeros((T + 1, D), x2d.dtype),
                    (slot_tok.reshape(n_blocks, EXPERT_BLOCK), slot_w.reshape(n_blocks, EXPERT_BLOCK), blk_expert))
    return y[:T]


def moe_ffn(x, layer, w_router, router_bias, w_exp_up, w_exp_down, w_shared_up, w_shared_down):
    B, S, D = x.shape
    x2d = x.reshape(B * S, D)
    idx, wts = route(x2d, w_router[layer], router_bias[layer])
    routed = routed_experts(x2d, idx, wts, layer, w_exp_up, w_exp_down)
    hs = x2d @ w_shared_up[layer]
    shared = (jax.nn.silu(hs[:, :D_EXPERT]) * hs[:, D_EXPERT:]) @ w_shared_down[layer]
    return (routed + shared).reshape(B, S, D)


def hybrid_layer(x, pos, shift_prev, wkv_prev, cache_kv, cache_kpe, page_table, layer,
                 w_in, mu_shift, decay_base, w_decay_up, iclr_base, w_iclr_up, w_gate_rw,
                 k_k, k_a, r_k, gn_g, gn_b, q_norm_g, w_q_b, kv_norm_g, w_uk, w_uv,
                 w_branch, w_out, ln_mix_g, ln_mix_b, w_router, router_bias,
                 w_exp_up, w_exp_down, w_shared_up, w_shared_down, ln_ffn_g, ln_ffn_b):
    B, S, D = x.shape
    proj = x @ w_in[layer]
    c1 = RW_COLS
    c2 = c1 + Q_RANK
    c3 = c2 + KV_RANK + QK_ROPE
    p_rw, p_q, p_kv, p_gate = jnp.split(proj, [c1, c2, c3], axis=-1)
    o_a, shift_new, wkv_new = rwkv7_branch(
        p_rw, shift_prev, wkv_prev, mu_shift[layer], decay_base[layer], w_decay_up[layer],
        iclr_base[layer], w_iclr_up[layer], w_gate_rw[layer], k_k[layer], k_a[layer], r_k[layer],
        gn_g[layer], gn_b[layer])
    q_lat, q_pe, c_kv, k_pe = mla_project(p_q, p_kv, pos, q_norm_g[layer], w_q_b[layer],
                                          kv_norm_g[layer], w_uk[layer])
    if cache_kv is None:
        o_lat = mla_attend_prompt(q_lat, q_pe, c_kv, k_pe)
    else:
        o_lat = mla_attend_sample(q_lat, q_pe, c_kv, k_pe, layer, cache_kv, cache_kpe, page_table)
    o_b = jnp.einsum('bshc,chd->bshd', o_lat, w_uv[layer]).reshape(B, S, MLA_WIDTH)
    branch = jnp.einsum('bsnc,ncd->bsnd', jnp.stack([o_a, o_b], axis=2), w_branch[layer])
    gates = jax.nn.sigmoid(p_gate.reshape(B, S, N_BRANCH, D))
    mix = jnp.sum(gates * branch, axis=2) @ w_out[layer]
    x = layer_norm(DN_ALPHA * x + mix, ln_mix_g[layer], ln_mix_b[layer])
    ffn = moe_ffn(x, layer, w_router, router_bias, w_exp_up, w_exp_down, w_shared_up, w_shared_down)
    x = layer_norm(DN_ALPHA * x + ffn, ln_ffn_g[layer], ln_ffn_b[layer])
    return x, c_kv, k_pe, wkv_new, shift_new


def setup_inputs(seed: int = 0) -> dict:
    key = jax.random.key(seed)
    ks = iter(jax.random.split(key, 48))
    f32 = jnp.float32

    def nrm(shape, scale):
        return jax.random.normal(next(ks), shape, f32) * scale

    def unif(shape, lo, hi):
        return jax.random.uniform(next(ks), shape, f32, lo, hi)

    L, D = DEPTH, D_MODEL
    n_pages = PAST_LEN // PAGE_SIZE
    in_use = DEC_BATCH * n_pages
    n_pool = in_use + max(1, in_use // 4)
    x_prompt = nrm((BATCH, SEQ, D), 1.0)
    x_sample = nrm((DEC_BATCH, DEC_SEQ, D), 1.0)
    cache_kv_latent = nrm((L, n_pool, PAGE_SIZE, KV_RANK), 1.0)
    cache_k_rope = nrm((L, n_pool, PAGE_SIZE, QK_ROPE), 1.0)
    state_wkv = nrm((L, DEC_BATCH, RW_HEADS, RW_HEAD_DIM, RW_HEAD_DIM), 0.3)
    state_shift = nrm((L, DEC_BATCH, RW_COLS), 1.0)
    page_table = jax.random.permutation(next(ks), n_pool)[:in_use].reshape(DEC_BATCH, n_pages).astype(jnp.int32)
    return {
        'x_prompt': x_prompt,
        'x_sample': x_sample,
        'cache_kv_latent': cache_kv_latent,
        'cache_k_rope': cache_k_rope,
        'state_wkv': state_wkv,
        'state_shift': state_shift,
        'page_table': page_table,
        'ln_in_g': 1.0 + nrm((D,), 0.05),
        'ln_in_b': nrm((D,), 0.02),
        'w_in': nrm((L, D, IN_COLS), D ** -0.5),
        'mu_shift': unif((L, RW_COLS), 0.0, 1.0),
        'decay_base': unif((L, RW_WIDTH), -6.0, -1.0),
        'w_decay_up': nrm((L, DECAY_RANK, RW_WIDTH), 0.1 * DECAY_RANK ** -0.5),
        'iclr_base': nrm((L, RW_WIDTH), 0.1),
        'w_iclr_up': nrm((L, ICLR_RANK, RW_WIDTH), 0.5 * ICLR_RANK ** -0.5),
        'w_gate_rw': nrm((L, GATE_RANK, RW_WIDTH), GATE_RANK ** -0.5),
        'k_k': 0.85 + nrm((L, RW_WIDTH), 0.05),
        'k_a': 1.0 + nrm((L, RW_WIDTH), 0.05),
        'r_k': nrm((L, RW_HEADS, RW_HEAD_DIM), 0.1),
        'gn_g': 1.0 + nrm((L, RW_WIDTH), 0.05),
        'gn_b': nrm((L, RW_WIDTH), 0.02),
        'q_norm_g': 1.0 + nrm((L, Q_RANK), 0.05),
        'w_q_b': nrm((L, Q_RANK, MLA_HEADS, QK_NOPE + QK_ROPE), Q_RANK ** -0.5),
        'kv_norm_g': 1.0 + nrm((L, KV_RANK), 0.05),
        'w_uk': nrm((L, KV_RANK, MLA_HEADS, QK_NOPE), KV_RANK ** -0.5),
        'w_uv': nrm((L, KV_RANK, MLA_HEADS, V_DIM), KV_RANK ** -0.5),
        'w_branch': nrm((L, N_BRANCH, BRANCH_WIDTH, D), DN_BETA * BRANCH_WIDTH ** -0.5),
        'w_out': nrm((L, D, D), DN_BETA * D ** -0.5),
        'ln_mix_g': 1.0 + nrm((L, D), 0.05),
        'ln_mix_b': nrm((L, D), 0.02),
        'w_router': nrm((L, D, N_EXPERTS), D ** -0.5),
        'router_bias': nrm((L, N_EXPERTS), 0.01),
        'w_exp_up': nrm((L, N_EXPERTS, D, 2 * D_EXPERT), DN_BETA * D ** -0.5),
        'w_exp_down': nrm((L, N_EXPERTS, D_EXPERT, D), DN_BETA * D_EXPERT ** -0.5),
        'w_shared_up': nrm((L, D, 2 * D_EXPERT), DN_BETA * D ** -0.5),
        'w_shared_down': nrm((L, D_EXPERT, D), DN_BETA * D_EXPERT ** -0.5),
        'ln_ffn_g': 1.0 + nrm((L, D), 0.05),
        'ln_ffn_b': nrm((L, D), 0.02),
    }


def reference(x_prompt, x_sample, cache_kv_latent, cache_k_rope, state_wkv, state_shift, page_table,
              ln_in_g, ln_in_b, w_in, mu_shift, decay_base, w_decay_up, iclr_base, w_iclr_up, w_gate_rw,
              k_k, k_a, r_k, gn_g, gn_b, q_norm_g, w_q_b, kv_norm_g, w_uk, w_uv, w_branch, w_out,
              ln_mix_g, ln_mix_b, w_router, router_bias, w_exp_up, w_exp_down, w_shared_up, w_shared_down,
              ln_ffn_g, ln_ffn_b):
    run = functools.partial(
        hybrid_layer, w_in=w_in, mu_shift=mu_shift, decay_base=decay_base, w_decay_up=w_decay_up,
        iclr_base=iclr_base, w_iclr_up=w_iclr_up, w_gate_rw=w_gate_rw, k_k=k_k, k_a=k_a, r_k=r_k,
        gn_g=gn_g, gn_b=gn_b, q_norm_g=q_norm_g, w_q_b=w_q_b, kv_norm_g=kv_norm_g, w_uk=w_uk, w_uv=w_uv,
        w_branch=w_branch, w_out=w_out, ln_mix_g=ln_mix_g, ln_mix_b=ln_mix_b, w_router=w_router,
        router_bias=router_bias, w_exp_up=w_exp_up, w_exp_down=w_exp_down, w_shared_up=w_shared_up,
        w_shared_down=w_shared_down, ln_ffn_g=ln_ffn_g, ln_ffn_b=ln_ffn_b)
    b_p = x_prompt.shape[0]
    pos_p = jnp.arange(x_prompt.shape[1], dtype=jnp.int32)
    pos_s = PAST_LEN + jnp.arange(x_sample.shape[1], dtype=jnp.int32)
    zero_shift = jnp.zeros((b_p, RW_COLS), x_prompt.dtype)
    zero_wkv = jnp.zeros((b_p, RW_HEADS, RW_HEAD_DIM, RW_HEAD_DIM), jnp.float32)
    h_p = layer_norm(x_prompt, ln_in_g, ln_in_b)
    h_s = layer_norm(x_sample, ln_in_g, ln_in_b)
    st_p, st_s = [], []
    for l in range(DEPTH):
        h_p, kv_p, kpe_p, wkv_p, sh_p = run(h_p, pos_p, zero_shift, zero_wkv, None, None, None, l)
        h_s, kv_s, kpe_s, wkv_s, sh_s = run(h_s, pos_s, state_shift[l], state_wkv[l],
                                            cache_kv_latent, cache_k_rope, page_table, l)
        st_p.append((kv_p, kpe_p, wkv_p, sh_p))
        st_s.append((kv_s, kpe_s, wkv_s, sh_s))
    new_kv_latent_prompt = jnp.stack([s[0] for s in st_p])
    new_k_rope_prompt = jnp.stack([s[1] for s in st_p])
    new_wkv_prompt = jnp.stack([s[2] for s in st_p])
    new_shift_prompt = jnp.stack([s[3] for s in st_p])
    new_kv_latent_sample = jnp.stack([s[0] for s in st_s])
    new_k_rope_sample = jnp.stack([s[1] for s in st_s])
    new_wkv_sample = jnp.stack([s[2] for s in st_s])
    new_shift_sample = jnp.stack([s[3] for s in st_s])
    return (h_p, h_s, new_kv_latent_prompt, new_k_rope_prompt, new_wkv_prompt, new_shift_prompt,
            new_kv_latent_sample, new_k_rope_sample, new_wkv_sample, new_shift_sample)
```

```python
import functools

import jax
import jax.numpy as jnp
from jax import lax
from jax.experimental import pallas as pl
from jax.experimental.pallas import tpu as pltpu

F32 = jnp.float32
BF16 = jnp.bfloat16

TOP_K = 8
N_GROUPS = 8
TOPK_GROUPS = 4
ROUTED_SCALE = 2.5
LN_EPS = 1e-5
RMS_EPS = 1e-6
GN_EPS = 64e-5
ROPE_THETA = 10000.0
KK_NORM_FLOOR = 1e-12

LANES = 128
SUBLANES = 8
VMEM_LIMIT = 56 * 1024 * 1024
HEAD = 64


def _cparams(*sem):
    return pltpu.CompilerParams(dimension_semantics=sem, vmem_limit_bytes=VMEM_LIMIT)


def _tile(n, want):
    t = min(n, want)
    while n % t:
        t -= 1
    return t


def _ln_math(x, g, b):
    mu = jnp.mean(x, axis=-1, keepdims=True)
    xc = x - mu
    var = jnp.mean(xc * xc, axis=-1, keepdims=True)
    return xc * lax.rsqrt(var + LN_EPS) * g + b


def _ln_kernel(x_ref, g_ref, b_ref, of_ref, ob_ref):
    y = _ln_math(x_ref[...], g_ref[...], b_ref[...])
    of_ref[...] = y
    ob_ref[...] = y.astype(BF16)


def layer_norm_rows(x, g, b):
    T, D = x.shape
    tm = _tile(T, 256)
    row = pl.BlockSpec((tm, D), lambda i: (i, 0))
    vec = pl.BlockSpec((1, D), lambda i: (0, 0))
    return pl.pallas_call(
        _ln_kernel,
        grid=(T // tm,),
        in_specs=[row, vec, vec],
        out_specs=[row, row],
        out_shape=[jax.ShapeDtypeStruct((T, D), F32), jax.ShapeDtypeStruct((T, D), BF16)],
        compiler_params=_cparams("parallel"),
        name="layer_norm_rows",
    )(x, g.reshape(1, D), b.reshape(1, D))


def _mm_kernel(x_ref, w_ref, o_ref):
    o_ref[...] = jnp.dot(x_ref[...], w_ref[...], preferred_element_type=F32).astype(o_ref.dtype)


def matmul(x, w, out_dtype, tm_want=1024, tn_want=512, name="matmul"):
    M, K = x.shape
    _, N = w.shape
    tm, tn = _tile(M, tm_want), _tile(N, tn_want)
    return pl.pallas_call(
        _mm_kernel,
        grid=(M // tm, N // tn),
        in_specs=[pl.BlockSpec((tm, K), lambda i, j: (i, 0)),
                  pl.BlockSpec((K, tn), lambda i, j: (0, j))],
        out_specs=pl.BlockSpec((tm, tn), lambda i, j: (i, j)),
        out_shape=jax.ShapeDtypeStruct((M, N), out_dtype),
        compiler_params=_cparams("parallel", "arbitrary"),
        name=name,
    )(x, w)


def _split3(x):
    hi = x.astype(BF16)
    r1 = x - hi.astype(F32)
    mid = r1.astype(BF16)
    lo = (r1 - mid.astype(F32)).astype(BF16)
    return hi, mid, lo


def _seg_sum(x, bd):
    pieces = _split3(x)
    cols = []
    for c in range(x.shape[1] // LANES):
        sl = slice(c * LANES, (c + 1) * LANES)
        acc = jnp.dot(pieces[0][:, sl], bd, preferred_element_type=F32)
        acc = acc + jnp.dot(pieces[1][:, sl], bd, preferred_element_type=F32)
        acc = acc + jnp.dot(pieces[2][:, sl], bd, preferred_element_type=F32)
        cols.append(acc)
    return jnp.concatenate(cols, axis=1) if len(cols) > 1 else cols[0]


def _block_diag_ones():
    r = jnp.arange(LANES) // HEAD
    return (r[:, None] == r[None, :]).astype(BF16)


def _rwkv_prep_kernel(rkv_ref, misc_ref, rkv_prev_ref, misc_prev_ref, sh_rkv_ref, sh_misc_ref,
                      mu_rkv_ref, mu_misc_ref, w0_ref, wdec_ref, a0_ref, wicl_ref, wg_ref,
                      kk_ref, ka_ref, bd_ref,
                      r_out, w_out, k_out, v_out, kkn_out, b_out, g_out, *, W, n_gate, n_dec, n_icl):
    i = pl.program_id(1)
    first = i == 0

    def shifted(p, prev8, sh, mu):
        prev_row = jnp.where(first, sh, prev8[SUBLANES - 1:SUBLANES, :])
        if p.shape[0] > 1:
            rolled = pltpu.roll(p, 1, axis=0)
            rid = lax.broadcasted_iota(jnp.int32, (p.shape[0], 1), 0)
            p_prev = jnp.where(rid == 0, prev_row, rolled)
        else:
            p_prev = prev_row
        return p + (p_prev - p) * mu

    rkv = shifted(rkv_ref[...], rkv_prev_ref[...], sh_rkv_ref[0], mu_rkv_ref[...])
    misc = shifted(misc_ref[...], misc_prev_ref[...], sh_misc_ref[0], mu_misc_ref[...])
    r = rkv[:, :W]
    k = rkv[:, W:2 * W]
    v = rkv[:, 2 * W:]
    xg = misc[:, :n_gate]
    xw = misc[:, n_gate:n_gate + n_dec]
    xa = misc[:, n_gate + n_dec:n_gate + n_dec + n_icl]

    dec_in = w0_ref[...] + jnp.dot(jnp.tanh(xw).astype(BF16), wdec_ref[...], preferred_element_type=F32)
    z = -dec_in
    softplus = jnp.maximum(z, 0.0) + jnp.log1p(jnp.exp(-jnp.abs(z)))
    w_log = -softplus - 0.5
    decay = jnp.exp(-jnp.exp(w_log))
    a = jax.nn.sigmoid(a0_ref[...] + jnp.dot(xa.astype(BF16), wicl_ref[...], preferred_element_type=F32))
    g = jnp.dot(jax.nn.sigmoid(xg).astype(BF16), wg_ref[...], preferred_element_type=F32)

    kk = k * kk_ref[...]
    ss = _seg_sum(kk * kk, bd_ref[...])
    kk = kk / jnp.maximum(jnp.sqrt(ss), KK_NORM_FLOOR)
    k2 = k * (1.0 + (a - 1.0) * ka_ref[...])

    r_out[...] = r
    w_out[...] = decay
    k_out[...] = k2
    v_out[...] = v
    kkn_out[...] = -kk
    b_out[...] = kk * a
    g_out[...] = g


def rwkv_prep(proj, row_off, B, S, lay, shift_rkv, shift_misc, prm):
    W = lay["W"]
    Tc = _tile(S, 256)
    nblk = S // Tc
    base = row_off // Tc
    base8 = row_off // SUBLANES
    per8 = Tc // SUBLANES
    mw = lay["misc_w"]
    mi = lay["misc_off"] // mw

    def cur(col):
        return lambda b, i: (base + b * nblk + i, col)

    def prev(col):
        return lambda b, i: (jnp.maximum(base8 + (b * nblk + i) * per8 - 1, 0), col)

    full = lambda shape: pl.BlockSpec(shape, lambda b, i: (0,) * len(shape))
    out_spec = pl.BlockSpec((Tc, W), lambda b, i: (b * nblk + i, 0))
    out_sds = jax.ShapeDtypeStruct((B * S, W), F32)
    kern = functools.partial(_rwkv_prep_kernel, W=W, n_gate=lay["n_gate"], n_dec=lay["n_dec"], n_icl=lay["n_icl"])
    return pl.pallas_call(
        kern,
        grid=(B, nblk),
        in_specs=[
            pl.BlockSpec((Tc, 3 * W), cur(0)),
            pl.BlockSpec((Tc, mw), cur(mi)),
            pl.BlockSpec((SUBLANES, 3 * W), prev(0)),
            pl.BlockSpec((SUBLANES, mw), prev(mi)),
            pl.BlockSpec((1, 1, 3 * W), lambda b, i: (b, 0, 0)),
            pl.BlockSpec((1, 1, mw), lambda b, i: (b, 0, 0)),
            full((1, 3 * W)), full((1, mw)),
            full((1, W)), full((lay["n_dec"], W)), full((1, W)), full((lay["n_icl"], W)),
            full((lay["n_gate"], W)), full((1, W)), full((1, W)), full((LANES, LANES)),
        ],
        out_specs=[out_spec] * 7,
        out_shape=[out_sds] * 7,
        compiler_params=_cparams("parallel", "arbitrary"),
        name="rwkv_prep",
    )(proj, proj, proj, proj, shift_rkv, shift_misc,
      prm["mu_rkv"], prm["mu_misc"], prm["w0"], prm["w_dec"], prm["a0"], prm["w_icl"], prm["w_g"],
      prm["k_k"], prm["k_a"], _block_diag_ones())


def _rwkv_scan_kernel(r_ref, w_ref, k_ref, v_ref, kk_ref, b_ref, s0_ref, y_ref, s_ref, *, Hb, Tc):
    t = pl.program_id(2)

    @pl.when(t == 0)
    def _():
        s_ref[...] = s0_ref[...]

    lane = lax.broadcasted_iota(jnp.int32, (HEAD, LANES), 1)

    def trip(i, carry):
        rows = pl.ds(pl.multiple_of(i * SUBLANES, SUBLANES), SUBLANES)
        r8, w8, k8, kk8, b8 = r_ref[rows, :], w_ref[rows, :], k_ref[rows, :], kk_ref[rows, :], b_ref[rows, :]
        v_t = v_ref[rows, :].T
        y_heads = []
        for h in range(Hb):
            cs = slice(h * HEAD, (h + 1) * HEAD)
            S = s_ref[0, h]
            y_mat = jnp.zeros((HEAD, LANES), F32)
            for s in range(SUBLANES):
                row = slice(s, s + 1)
                sa = jnp.sum(S * kk8[row, cs], axis=1, keepdims=True)
                S = S * w8[row, cs] + sa * b8[row, cs] + v_t[cs, s:s + 1] * k8[row, cs]
                y_col = jnp.sum(S * r8[row, cs], axis=1, keepdims=True)
                y_mat = jnp.where(lane == s, y_col, y_mat)
            s_ref[0, h] = S
            y_heads.append(y_mat)
        y_all = jnp.concatenate(y_heads, axis=0) if Hb > 1 else y_heads[0]
        y_ref[rows, :] = y_all.T[:SUBLANES, :]
        return carry

    lax.fori_loop(0, Tc // SUBLANES, trip, 0)


def rwkv_scan(r, w, k, v, kkn, b, s0, B, S, H):
    W = H * HEAD
    Hb = _tile(H, 8)
    Tc = _tile(S, 256)
    nt = S // Tc
    seq = pl.BlockSpec((Tc, Hb * HEAD), lambda bi, hb, t: (bi * nt + t, hb))
    st = pl.BlockSpec((1, Hb, HEAD, HEAD), lambda bi, hb, t: (bi, hb, 0, 0))
    kern = functools.partial(_rwkv_scan_kernel, Hb=Hb, Tc=Tc)
    return pl.pallas_call(
        kern,
        grid=(B, H // Hb, nt),
        in_specs=[seq] * 6 + [st],
        out_specs=[seq, st],
        out_shape=[jax.ShapeDtypeStruct((B * S, W), F32), jax.ShapeDtypeStruct((B, H, HEAD, HEAD), F32)],
        compiler_params=_cparams("parallel", "parallel", "arbitrary"),
        name="rwkv_scan",
    )(r, w, k, v, kkn, b, s0)


def _rwkv_post_kernel(y_ref, r_ref, k_ref, v_ref, g_ref, rk_ref, gng_ref, gnb_ref, bd_ref, o_ref):
    bd = bd_ref[...]
    y = y_ref[...]
    inv_n = 1.0 / HEAD
    y_mu = _seg_sum(y, bd) * inv_n
    yc = y - y_mu
    y_var = _seg_sum(yc * yc, bd) * inv_n
    yn = yc * lax.rsqrt(y_var + GN_EPS) * gng_ref[...] + gnb_ref[...]
    bonus = _seg_sum(r_ref[...] * k_ref[...] * rk_ref[...], bd) * v_ref[...]
    o_ref[...] = ((yn + bonus) * g_ref[...]).astype(o_ref.dtype)


def rwkv_post(y, r, k2, v, g, prm):
    T, W = y.shape
    tm = _tile(T, 256)
    row = pl.BlockSpec((tm, W), lambda i: (i, 0))
    vec = pl.BlockSpec((1, W), lambda i: (0, 0))
    return pl.pallas_call(
        _rwkv_post_kernel,
        grid=(T // tm,),
        in_specs=[row] * 5 + [vec] * 3 + [pl.BlockSpec((LANES, LANES), lambda i: (0, 0))],
        out_specs=row,
        out_shape=jax.ShapeDtypeStruct((T, W), BF16),
        compiler_params=_cparams("parallel"),
        name="rwkv_post",
    )(y, r, k2, v, g, prm["r_k"], prm["gn_g"], prm["gn_b"], _block_diag_ones())


def _swap_halves(x, width):
    n = x.shape[1]
    half = width // 2
    lane = lax.broadcasted_iota(jnp.int32, x.shape, 1)
    fwd = pltpu.roll(x, half, axis=1)
    bwd = pltpu.roll(x, n - half, axis=1)
    return jnp.where(lane % width < half, bwd, fwd)


def _mla_prep_kernel(pq_ref, ckv_ref, misc_ref, qg_ref, kvg_ref, cos_ref, sin_ref,
                     qn_ref, ckv_f_ref, ckv_b_ref, kpe_f_ref, kpe_b_ref, *, pe_off, rope):
    def rms(x, g):
        return x * lax.rsqrt(jnp.mean(x * x, axis=-1, keepdims=True) + RMS_EPS) * g

    qn_ref[...] = rms(pq_ref[...], qg_ref[...]).astype(BF16)
    ckv = rms(ckv_ref[...], kvg_ref[...])
    ckv_f_ref[...] = ckv
    ckv_b_ref[...] = ckv.astype(BF16)
    blk = misc_ref[:, pe_off - pe_off % LANES: pe_off - pe_off % LANES + LANES]
    roped = blk * cos_ref[...] + _swap_halves(blk, rope) * sin_ref[...]
    kpe = roped[:, pe_off % LANES: pe_off % LANES + rope]
    kpe_f_ref[...] = kpe
    kpe_b_ref[...] = kpe.astype(BF16)


def mla_prep(proj, lay, q_norm_g, kv_norm_g, cos_t, sin_t):
    T = proj.shape[0]
    QR, KR, rope, mw = lay["q_rank"], lay["kv_rank"], lay["rope"], lay["misc_w"]
    tm = _tile(T, 256)
    row = lambda wd, col: pl.BlockSpec((tm, wd), lambda i: (i, col))
    vec = lambda wd: pl.BlockSpec((1, wd), lambda i: (0, 0))
    kern = functools.partial(_mla_prep_kernel, pe_off=lay["pe_in_misc"], rope=rope)
    return pl.pallas_call(
        kern,
        grid=(T // tm,),
        in_specs=[row(QR, lay["q_off"] // QR), row(KR, lay["ckv_off"] // KR), row(mw, lay["misc_off"] // mw),
                  vec(QR), vec(KR), row(LANES, 0), row(LANES, 0)],
        out_specs=[row(QR, 0), row(KR, 0), row(KR, 0), row(rope, 0), row(rope, 0)],
        out_shape=[jax.ShapeDtypeStruct((T, QR), BF16), jax.ShapeDtypeStruct((T, KR), F32),
                   jax.ShapeDtypeStruct((T, KR), BF16), jax.ShapeDtypeStruct((T, rope), F32),
                   jax.ShapeDtypeStruct((T, rope), BF16)],
        compiler_params=_cparams("parallel"),
        name="mla_prep",
    )(proj, proj, proj, q_norm_g.reshape(1, QR), kv_norm_g.reshape(1, KR), cos_t, sin_t)


def _q_absorb_kernel(nope_ref, pe_ref, wuk_ref, cos_ref, sin_ref, ql_ref, qp_ref, *, nope, rope, scale):
    hp = LANES // rope
    for h in range(hp):
        qn = nope_ref[:, h * nope:(h + 1) * nope].astype(BF16)
        ql = jnp.dot(qn, wuk_ref[h], preferred_element_type=F32) * scale
        ql_ref[h] = ql.astype(ql_ref.dtype)
    pe = pe_ref[...]
    roped = (pe * cos_ref[...] + _swap_halves(pe, rope) * sin_ref[...]) * scale
    for h in range(hp):
        qp_ref[h] = roped[:, h * rope:(h + 1) * rope].astype(qp_ref.dtype)


def q_absorb(qa, row_off, Tg, H, nope, rope, kv_rank, w_uk_t, cos_t, sin_t, scale, out_dtype):
    hp = LANES // rope
    tm = _tile(Tg, 256)
    base = row_off // tm
    pe_col0 = (H * nope) // LANES
    kern = functools.partial(_q_absorb_kernel, nope=nope, rope=rope, scale=scale)
    return pl.pallas_call(
        kern,
        grid=(Tg // tm, H // hp),
        in_specs=[pl.BlockSpec((tm, hp * nope), lambda i, j: (base + i, j)),
                  pl.BlockSpec((tm, LANES), lambda i, j: (base + i, pe_col0 + j)),
                  pl.BlockSpec((hp, nope, kv_rank), lambda i, j: (j, 0, 0)),
                  pl.BlockSpec((tm, LANES), lambda i, j: (base + i, 0)),
                  pl.BlockSpec((tm, LANES), lambda i, j: (base + i, 0))],
        out_specs=[pl.BlockSpec((hp, tm, kv_rank), lambda i, j: (j, i, 0)),
                   pl.BlockSpec((hp, tm, rope), lambda i, j: (j, i, 0))],
        out_shape=[jax.ShapeDtypeStruct((H, Tg, kv_rank), out_dtype),
                   jax.ShapeDtypeStruct((H, Tg, rope), out_dtype)],
        compiler_params=_cparams("parallel", "arbitrary"),
        name="q_absorb",
    )(qa, qa, w_uk_t, cos_t, sin_t)


_NT = (((1,), (1,)), ((), ()))


def _attn_prompt_kernel(ql_ref, qp_ref, ckv_ref, kpe_ref, wuv_ref, o_ref, m_ref, l_ref, acc_ref, *, H, TQ, CK, vdim):
    i = pl.program_id(1)
    R = H * TQ
    q = ql_ref[...].reshape(R, ql_ref.shape[-1])
    qp = qp_ref[...].reshape(R, qp_ref.shape[-1])
    m_ref[...] = jnp.full(m_ref.shape, -jnp.inf, F32)
    l_ref[...] = jnp.zeros(l_ref.shape, F32)
    acc_ref[...] = jnp.zeros(acc_ref.shape, F32)
    q_pos = i * TQ + lax.broadcasted_iota(jnp.int32, (R, 1), 0) % TQ
    n_chunks = ((i + 1) * TQ + CK - 1) // CK

    def chunk(j, carry):
        ks = pl.ds(pl.multiple_of(j * CK, CK), CK)
        kc = ckv_ref[ks, :]
        s = lax.dot_general(q, kc, _NT, preferred_element_type=F32)
        s = s + lax.dot_general(qp, kpe_ref[ks, :], _NT, preferred_element_type=F32)
        key_pos = j * CK + lax.broadcasted_iota(jnp.int32, (1, CK), 1)
        s = jnp.where(key_pos <= q_pos, s, -jnp.inf)
        m_old = m_ref[...]
        m_new = jnp.maximum(m_old, jnp.max(s, axis=-1, keepdims=True))
        corr = jnp.exp(m_old - m_new)
        p = jnp.exp(s - m_new)
        l_ref[...] = l_ref[...] * corr + jnp.sum(p, axis=-1, keepdims=True)
        acc_ref[...] = acc_ref[...] * corr + jnp.dot(p.astype(BF16), kc, preferred_element_type=F32)
        m_ref[...] = m_new
        return carry

    lax.fori_loop(0, n_chunks, chunk, 0)
    o = (acc_ref[...] / l_ref[...]).astype(BF16)
    for h in range(H):
        o_ref[:, h * vdim:(h + 1) * vdim] = jnp.dot(
            o[h * TQ:(h + 1) * TQ, :], wuv_ref[h], preferred_element_type=F32).astype(o_ref.dtype)


def attn_prompt(q_lat, q_pe, ckv_b, kpe_b, w_uv_t, B, S):
    H, _, KR = q_lat.shape
    rope = q_pe.shape[-1]
    vdim = w_uv_t.shape[-1]
    TQ = _tile(S, 64)
    CK = _tile(S, 512)
    nq = S // TQ
    kern = functools.partial(_attn_prompt_kernel, H=H, TQ=TQ, CK=CK, vdim=vdim)
    return pl.pallas_call(
        kern,
        grid=(B, nq),
        in_specs=[pl.BlockSpec((H, TQ, KR), lambda b, i: (0, b * nq + i, 0)),
                  pl.BlockSpec((H, TQ, rope), lambda b, i: (0, b * nq + i, 0)),
                  pl.BlockSpec((S, KR), lambda b, i: (b, 0)),
                  pl.BlockSpec((S, rope), lambda b, i: (b, 0)),
                  pl.BlockSpec((H, KR, vdim), lambda b, i: (0, 0, 0))],
        out_specs=pl.BlockSpec((TQ, H * vdim), lambda b, i: (b * nq + i, 0)),
        out_shape=jax.ShapeDtypeStruct((B * S, H * vdim), BF16),
        scratch_shapes=[pltpu.VMEM((H * TQ, 1), F32), pltpu.VMEM((H * TQ, 1), F32),
                        pltpu.VMEM((H * TQ, KR), F32)],
        compiler_params=_cparams("parallel", "arbitrary"),
        name="attn_prompt",
    )(q_lat, q_pe, ckv_b, kpe_b, w_uv_t)


def _attn_sample_kernel(pt_ref, ql_ref, qp_ref, ckv_ref, kpe_ref, wuv_ref, *rest, H, TS, G, vdim):
    kv_refs = rest[:G]
    kp_refs = rest[G:2 * G]
    o_ref, m_ref, l_ref, acc_ref = rest[2 * G:]
    g = pl.program_id(1)
    R = H * TS
    q32 = ql_ref[...].reshape(R, ql_ref.shape[-1])
    qp32 = qp_ref[...].reshape(R, qp_ref.shape[-1])
    q = q32.astype(BF16)
    qp = qp32.astype(BF16)

    @pl.when(g == 0)
    def _():
        ck = ckv_ref[...]
        s = lax.dot_general(q32, ck, _NT, preferred_element_type=F32)
        s = s + lax.dot_general(qp32, kpe_ref[...], _NT, preferred_element_type=F32)
        t_row = lax.broadcasted_iota(jnp.int32, (R, 1), 0) % TS
        t_key = lax.broadcasted_iota(jnp.int32, (1, TS), 1)
        s = jnp.where(t_key <= t_row, s, -jnp.inf)
        m0 = jnp.max(s, axis=-1, keepdims=True)
        p = jnp.exp(s - m0)
        m_ref[...] = m0
        l_ref[...] = jnp.sum(p, axis=-1, keepdims=True)
        acc_ref[...] = jnp.dot(p, ck, preferred_element_type=F32)

    pages = [kv_refs[j][0, 0].astype(BF16) for j in range(G)]
    s_parts = []
    for j in range(G):
        sj = lax.dot_general(q, pages[j], _NT, preferred_element_type=F32)
        sj = sj + lax.dot_general(qp, kp_refs[j][0, 0].astype(BF16), _NT, preferred_element_type=F32)
        s_parts.append(sj)
    s = jnp.concatenate(s_parts, axis=1) if G > 1 else s_parts[0]
    m_old = m_ref[...]
    m_new = jnp.maximum(m_old, jnp.max(s, axis=-1, keepdims=True))
    corr = jnp.exp(m_old - m_new)
    p = jnp.exp(s - m_new)
    l_ref[...] = l_ref[...] * corr + jnp.sum(p, axis=-1, keepdims=True)
    pb = p.astype(BF16)
    P = pages[0].shape[0]
    pv = jnp.dot(pb[:, :P], pages[0], preferred_element_type=F32)
    for j in range(1, G):
        pv = pv + jnp.dot(pb[:, j * P:(j + 1) * P], pages[j], preferred_element_type=F32)
    acc_ref[...] = acc_ref[...] * corr + pv
    m_ref[...] = m_new

    @pl.when(g == pl.num_programs(1) - 1)
    def _():
        o = (acc_ref[...] / l_ref[...]).astype(BF16)
        for h in range(H):
            o_ref[:, h * vdim:(h + 1) * vdim] = jnp.dot(
                o[h * TS:(h + 1) * TS, :], wuv_ref[h], preferred_element_type=F32).astype(o_ref.dtype)


def attn_sample(q_lat, q_pe, ckv_f, kpe_f, row_off, w_uv_t, cache_kv, cache_kpe, page_table, layer, DB, TS):
    H, _, KR = q_lat.shape
    rope = q_pe.shape[-1]
    vdim = w_uv_t.shape[-1]
    n_pages = page_table.shape[1]
    P = cache_kv.shape[2]
    G = _tile(n_pages, 8)
    base = row_off // TS

    def page_spec(width, j):
        return pl.BlockSpec((1, 1, P, width), lambda b, g, pt: (layer, pt[b, g * G + j], 0, 0))

    kern = functools.partial(_attn_sample_kernel, H=H, TS=TS, G=G, vdim=vdim)
    grid_spec = pltpu.PrefetchScalarGridSpec(
        num_scalar_prefetch=1,
        grid=(DB, n_pages // G),
        in_specs=[pl.BlockSpec((H, TS, KR), lambda b, g, pt: (0, b, 0)),
                  pl.BlockSpec((H, TS, rope), lambda b, g, pt: (0, b, 0)),
                  pl.BlockSpec((TS, KR), lambda b, g, pt: (base + b, 0)),
                  pl.BlockSpec((TS, rope), lambda b, g, pt: (base + b, 0)),
                  pl.BlockSpec((H, KR, vdim), lambda b, g, pt: (0, 0, 0))]
                 + [page_spec(KR, j) for j in range(G)] + [page_spec(rope, j) for j in range(G)],
        out_specs=pl.BlockSpec((TS, H * vdim), lambda b, g, pt: (b, 0)),
        scratch_shapes=[pltpu.VMEM((H * TS, 1), F32), pltpu.VMEM((H * TS, 1), F32),
                        pltpu.VMEM((H * TS, KR), F32)],
    )
    return pl.pallas_call(
        kern,
        grid_spec=grid_spec,
        out_shape=jax.ShapeDtypeStruct((DB * TS, H * vdim), BF16),
        compiler_params=_cparams("parallel", "arbitrary"),
        name="attn_sample",
    )(page_table, q_lat, q_pe, ckv_f, kpe_f, w_uv_t, *([cache_kv] * G), *([cache_kpe] * G))


def _merge_kernel(oa_ref, ob_ref, wa_ref, wb_ref, ga_ref, gb_ref, o_ref):
    a = jnp.dot(oa_ref[...], wa_ref[...], preferred_element_type=F32)
    b = jnp.dot(ob_ref[...], wb_ref[...], preferred_element_type=F32)
    o_ref[...] = (jax.nn.sigmoid(ga_ref[...]) * a + jax.nn.sigmoid(gb_ref[...]) * b).astype(o_ref.dtype)


def gated_merge(o_a, o_b, w_branch_b, proj, gate_off, D):
    T, Wb = o_a.shape
    tm, tn = _tile(T, 1024), _tile(D, 512)
    g0 = gate_off // tn
    g1 = (gate_off + D) // tn
    return pl.pallas_call(
        _merge_kernel,
        grid=(T // tm, D // tn),
        in_specs=[pl.BlockSpec((tm, Wb), lambda i, j: (i, 0)),
                  pl.BlockSpec((tm, Wb), lambda i, j: (i, 0)),
                  pl.BlockSpec((None, Wb, tn), lambda i, j: (0, 0, j)),
                  pl.BlockSpec((None, Wb, tn), lambda i, j: (1, 0, j)),
                  pl.BlockSpec((tm, tn), lambda i, j: (i, g0 + j)),
                  pl.BlockSpec((tm, tn), lambda i, j: (i, g1 + j))],
        out_specs=pl.BlockSpec((tm, tn), lambda i, j: (i, j)),
        out_shape=jax.ShapeDtypeStruct((T, D), BF16),
        compiler_params=_cparams("parallel", "arbitrary"),
        name="gated_merge",
    )(o_a, o_b, w_branch_b, w_branch_b, proj, proj)


def _mm_resid_kernel(x_ref, w_ref, h_ref, o_ref, *, alpha):
    o_ref[...] = alpha * h_ref[...] + jnp.dot(x_ref[...], w_ref[...], preferred_element_type=F32)


def matmul_residual(x, w, h, alpha):
    M, K = x.shape
    N = w.shape[1]
    tm, tn = _tile(M, 1024), _tile(N, 512)
    return pl.pallas_call(
        functools.partial(_mm_resid_kernel, alpha=alpha),
        grid=(M // tm, N // tn),
        in_specs=[pl.BlockSpec((tm, K), lambda i, j: (i, 0)),
                  pl.BlockSpec((K, tn), lambda i, j: (0, j)),
                  pl.BlockSpec((tm, tn), lambda i, j: (i, j))],
        out_specs=pl.BlockSpec((tm, tn), lambda i, j: (i, j)),
        out_shape=jax.ShapeDtypeStruct((M, N), F32),
        compiler_params=_cparams("parallel", "arbitrary"),
        name="matmul_residual",
    )(x, w, h)


def _router_kernel(x_ref, wh_ref, wl_ref, s_ref):
    x = x_ref[...]
    xh = x.astype(BF16)
    xl = (x - xh.astype(F32)).astype(BF16)
    wh = wh_ref[...]
    logits = jnp.dot(xh, wh, preferred_element_type=F32)
    logits = logits + jnp.dot(xl, wh, preferred_element_type=F32)
    logits = logits + jnp.dot(xh, wl_ref[...], preferred_element_type=F32)
    s_ref[...] = jax.nn.sigmoid(logits)


def router_scores(x, w_router):
    T, D = x.shape
    E = w_router.shape[1]
    wh = w_router.astype(BF16)
    wl = (w_router - wh.astype(F32)).astype(BF16)
    tm = _tile(T, 512)
    return pl.pallas_call(
        _router_kernel,
        grid=(T // tm,),
        in_specs=[pl.BlockSpec((tm, D), lambda i: (i, 0)),
                  pl.BlockSpec((D, E), lambda i: (0, 0)),
                  pl.BlockSpec((D, E), lambda i: (0, 0))],
        out_specs=pl.BlockSpec((tm, E), lambda i: (i, 0)),
        out_shape=jax.ShapeDtypeStruct((T, E), F32),
        compiler_params=_cparams("parallel"),
        name="router_scores",
    )(x, wh, wl)


def _expert_kernel(be_ref, nu_ref, x_ref, wg_ref, wu_ref, wd_ref, o_ref):
    blk = pl.program_id(0)
    j = pl.program_id(1)

    @pl.when(blk < nu_ref[0])
    def _():
        x = x_ref[...]
        hg = jnp.dot(x, wg_ref[0, 0].astype(BF16), preferred_element_type=F32)
        hu = jnp.dot(x, wu_ref[0, 0].astype(BF16), preferred_element_type=F32)
        act = (hg * jax.nn.sigmoid(hg) * hu).astype(BF16)
        part = jnp.dot(act, wd_ref[0, 0].astype(BF16), preferred_element_type=F32)

        @pl.when(j == 0)
        def _():
            o_ref[...] = part

        @pl.when(j > 0)
        def _():
            o_ref[...] += part


def expert_ffn(x_sorted, blk_expert, n_used, w_exp_up, w_exp_down, layer, bm):
    n_slots, D = x_sorted.shape
    DE = w_exp_down.shape[2]
    J = 2 if DE % (2 * LANES) == 0 else 1
    dh = DE // J
    n_blocks = n_slots // bm

    def live(blk, nu):
        return jnp.minimum(blk, nu[0] - 1)

    def jj(blk, j, nu):
        return jnp.where(blk < nu[0], j, J - 1)

    grid_spec = pltpu.PrefetchScalarGridSpec(
        num_scalar_prefetch=2,
        grid=(n_blocks, J),
        in_specs=[pl.BlockSpec((bm, D), lambda b, j, be, nu: (live(b, nu), 0)),
                  pl.BlockSpec((1, 1, D, dh), lambda b, j, be, nu: (layer, be[live(b, nu)], 0, jj(b, j, nu))),
                  pl.BlockSpec((1, 1, D, dh), lambda b, j, be, nu: (layer, be[live(b, nu)], 0, J + jj(b, j, nu))),
                  pl.BlockSpec((1, 1, dh, D), lambda b, j, be, nu: (layer, be[live(b, nu)], jj(b, j, nu), 0))],
        out_specs=pl.BlockSpec((bm, D), lambda b, j, be, nu: (live(b, nu), 0)),
    )
    return pl.pallas_call(
        _expert_kernel,
        grid_spec=grid_spec,
        out_shape=jax.ShapeDtypeStruct((n_slots, D), F32),
        compiler_params=_cparams("arbitrary", "arbitrary"),
        name="expert_ffn",
    )(blk_expert, n_used, x_sorted, w_exp_up, w_exp_up, w_exp_down)


def _shared_final_kernel(xb_ref, x_ref, routed_ref, wu_ref, wd_ref, g_ref, b_ref, o_ref, *, alpha, DE):
    hs = jnp.dot(xb_ref[...], wu_ref[...], preferred_element_type=F32)
    hg, hu = hs[:, :DE], hs[:, DE:]
    act = (hg * jax.nn.sigmoid(hg) * hu).astype(BF16)
    shared = jnp.dot(act, wd_ref[...], preferred_element_type=F32)
    pre = alpha * x_ref[...] + (routed_ref[...] + shared)
    o_ref[...] = _ln_math(pre, g_ref[...], b_ref[...])


def shared_ffn_final(xb, x, routed, w_up_b, w_down_b, g, b, alpha):
    T, D = x.shape
    DE = w_down_b.shape[0]
    tm = _tile(T, 128)
    row = lambda: pl.BlockSpec((tm, D), lambda i: (i, 0))
    vec = pl.BlockSpec((1, D), lambda i: (0, 0))
    return pl.pallas_call(
        functools.partial(_shared_final_kernel, alpha=alpha, DE=DE),
        grid=(T // tm,),
        in_specs=[row(), row(), row(),
                  pl.BlockSpec((D, 2 * DE), lambda i: (0, 0)),
                  pl.BlockSpec((DE, D), lambda i: (0, 0)), vec, vec],
        out_specs=row(),
        out_shape=jax.ShapeDtypeStruct((T, D), F32),
        compiler_params=_cparams("parallel"),
        name="shared_ffn_final",
    )(xb, x, routed, w_up_b, w_down_b, g.reshape(1, D), b.reshape(1, D))


def _route(s, router_bias):
    T, E = s.shape
    sc = s + router_bias.astype(F32)
    grp = jnp.sum(lax.top_k(sc.reshape(T, N_GROUPS, E // N_GROUPS), 2)[0], axis=-1)
    _, gidx = lax.top_k(grp, TOPK_GROUPS)
    gmask = jnp.sum(jax.nn.one_hot(gidx, N_GROUPS), axis=1) > 0
    emask = jnp.repeat(gmask, E // N_GROUPS, axis=1)
    _, idx = lax.top_k(jnp.where(emask, sc, -jnp.inf), TOP_K)
    wts = jnp.take_along_axis(s, idx, axis=1)
    wts = wts / jnp.sum(wts, axis=-1, keepdims=True) * ROUTED_SCALE
    return idx, wts


def _dispatch_plan(idx, E, bm):
    T, K = idx.shape
    M = T * K
    n_blocks = -(-M // bm) + E
    flat_e = idx.reshape(M)
    order = jnp.argsort(flat_e)
    e_sorted = flat_e[order]
    tok_sorted = (order // K).astype(jnp.int32)
    counts = jnp.bincount(flat_e, length=E)
    blocks_per_e = (counts + bm - 1) // bm
    blk_end = jnp.cumsum(blocks_per_e)
    blk_start = blk_end - blocks_per_e
    row_start = jnp.cumsum(counts) - counts
    dest = (blk_start[e_sorted] * bm + jnp.arange(M) - row_start[e_sorted]).astype(jnp.int32)
    slot_tok = jnp.full((n_blocks * bm,), T, jnp.int32).at[dest].set(tok_sorted)
    slot_of = jnp.zeros((M,), jnp.int32).at[order].set(dest).reshape(T, K)
    blk_expert = jnp.minimum(jnp.searchsorted(blk_end, jnp.arange(n_blocks), side="right"), E - 1).astype(jnp.int32)
    n_used = blk_end[-1:].astype(jnp.int32)
    return slot_tok, slot_of, blk_expert, n_used


def _in_layout(W, n_dec, n_icl, n_gate, q_rank, kv_rank, rope, D):
    misc_w = -(-(n_gate + n_dec + n_icl + rope) // LANES) * LANES
    lay = dict(W=W, n_dec=n_dec, n_icl=n_icl, n_gate=n_gate, q_rank=q_rank, kv_rank=kv_rank, rope=rope,
               misc_w=misc_w)
    off = 3 * W
    lay["gate_off"] = off
    off += 2 * D
    lay["q_off"] = off
    off += q_rank
    lay["ckv_off"] = off
    off += kv_rank
    lay["misc_off"] = off
    off += misc_w
    lay["ncols"] = off
    lay["pe_in_misc"] = n_gate + n_dec + n_icl
    return lay


def _permute_in_cols(w, lay, D):
    W, nd, ni, ng = lay["W"], lay["n_dec"], lay["n_icl"], lay["n_gate"]
    c1 = 3 * W + nd + ni + ng
    c2 = c1 + lay["q_rank"]
    c3 = c2 + lay["kv_rank"] + lay["rope"]
    rkv = w[..., :3 * W]
    xw = w[..., 3 * W:3 * W + nd]
    xa = w[..., 3 * W + nd:3 * W + nd + ni]
    xg = w[..., 3 * W + nd + ni:c1]
    q = w[..., c1:c2]
    ckv = w[..., c2:c2 + lay["kv_rank"]]
    kpe = w[..., c2 + lay["kv_rank"]:c3]
    gates = w[..., c3:]
    pad = jnp.zeros(w.shape[:-1] + (lay["misc_w"] - (ng + nd + ni + lay["rope"]),), w.dtype)
    return jnp.concatenate([rkv, gates, q, ckv, xg, xw, xa, kpe, pad], axis=-1)


def _rope_tables(pos, rope):
    half = rope // 2
    inv = ROPE_THETA ** (-jnp.arange(half, dtype=F32) / half)
    ang = pos.astype(F32)[:, None] * inv[None, :]
    cos, sin = jnp.cos(ang), jnp.sin(ang)
    reps = LANES // rope
    cos_t = jnp.tile(jnp.concatenate([cos, cos], axis=-1), (1, reps))
    sin_t = jnp.tile(jnp.concatenate([-sin, sin], axis=-1), (1, reps))
    return cos_t, sin_t


def kernel(x_prompt, x_sample, cache_kv_latent, cache_k_rope, state_wkv, state_shift, page_table,
           ln_in_g, ln_in_b, w_in, mu_shift, decay_base, w_decay_up, iclr_base, w_iclr_up, w_gate_rw,
           k_k, k_a, r_k, gn_g, gn_b, q_norm_g, w_q_b, kv_norm_g, w_uk, w_uv, w_branch, w_out,
           ln_mix_g, ln_mix_b, w_router, router_bias, w_exp_up, w_exp_down, w_shared_up, w_shared_down,
           ln_ffn_g, ln_ffn_b):
    Bp, Sp, D = x_prompt.shape
    Bs, Ss, _ = x_sample.shape
    L = w_in.shape[0]
    Tp, Ts = Bp * Sp, Bs * Ss
    T = Tp + Ts
    H_rw = r_k.shape[1]
    W = H_rw * HEAD
    n_dec, n_icl, n_gate = w_decay_up.shape[1], w_iclr_up.shape[1], w_gate_rw.shape[1]
    q_rank, H_mla, qk = w_q_b.shape[1], w_q_b.shape[2], w_q_b.shape[3]
    kv_rank, nope = w_uk.shape[1], w_uk.shape[3]
    rope = qk - nope
    vdim = w_uv.shape[3]
    E = w_router.shape[2]
    DE = w_exp_down.shape[2]
    past_len = page_table.shape[1] * cache_kv_latent.shape[2]
    alpha = (2.0 * L) ** 0.25
    scale = float(qk) ** -0.5
    lay = _in_layout(W, n_dec, n_icl, n_gate, q_rank, kv_rank, rope, D)

    pos = jnp.concatenate([jnp.tile(jnp.arange(Sp, dtype=jnp.int32), Bp),
                           jnp.tile(past_len + jnp.arange(Ss, dtype=jnp.int32), Bs)])
    cos_t, sin_t = _rope_tables(pos, rope)

    x_all = jnp.concatenate([x_prompt.reshape(Tp, D), x_sample.reshape(Ts, D)], axis=0)
    h, hb = layer_norm_rows(x_all, ln_in_g, ln_in_b)

    def perm_rw(vec):
        rkv = vec[..., :3 * W]
        xw = vec[..., 3 * W:3 * W + n_dec]
        xa = vec[..., 3 * W + n_dec:3 * W + n_dec + n_icl]
        xg = vec[..., 3 * W + n_dec + n_icl:]
        pad = jnp.zeros(vec.shape[:-1] + (lay["misc_w"] - (n_gate + n_dec + n_icl),), vec.dtype)
        return rkv, jnp.concatenate([xg, xw, xa, pad], axis=-1)

    st_p, st_s = [], []
    for l in range(L):
        w_in_b = _permute_in_cols(w_in[l], lay, D).astype(BF16)
        proj = matmul(hb, w_in_b, F32, name="in_proj")

        mu_rkv, mu_misc = perm_rw(mu_shift[l][None, :])
        rw_prm = dict(
            mu_rkv=mu_rkv, mu_misc=mu_misc, w0=decay_base[l][None, :], w_dec=w_decay_up[l].astype(BF16),
            a0=iclr_base[l][None, :], w_icl=w_iclr_up[l].astype(BF16), w_g=w_gate_rw[l].astype(BF16),
            k_k=k_k[l][None, :], k_a=k_a[l][None, :], r_k=r_k[l].reshape(1, W),
            gn_g=gn_g[l][None, :], gn_b=gn_b[l][None, :])

        def rwkv_group(row_off, B, S, shift_prev, wkv_prev):
            sh_rkv, sh_misc = perm_rw(shift_prev)
            r, dec, k2, v, kkn, bb, g = rwkv_prep(proj, row_off, B, S, lay, sh_rkv[:, None, :],
                                                  sh_misc[:, None, :], rw_prm)
            y, wkv_new = rwkv_scan(r, dec, k2, v, kkn, bb, wkv_prev, B, S, H_rw)
            o_a = rwkv_post(y, r, k2, v, g, rw_prm)
            last = proj[row_off + jnp.arange(B) * S + (S - 1)]
            mo = lay["misc_off"]
            shift_new = jnp.concatenate(
                [last[:, :3 * W], last[:, mo + n_gate:mo + n_gate + n_dec + n_icl], last[:, mo:mo + n_gate]], axis=-1)
            return o_a, wkv_new, shift_new

        zero_shift = jnp.zeros((Bp, mu_shift.shape[1]), F32)
        zero_wkv = jnp.zeros((Bp, H_rw, HEAD, HEAD), F32)
        oa_p, wkv_p, sh_p = rwkv_group(0, Bp, Sp, zero_shift, zero_wkv)
        oa_s, wkv_s, sh_s = rwkv_group(Tp, Bs, Ss, state_shift[l], state_wkv[l])
        o_a = jnp.concatenate([oa_p, oa_s], axis=0)

        qn, ckv_f, ckv_b, kpe_f, kpe_b = mla_prep(proj, lay, q_norm_g[l], kv_norm_g[l], cos_t, sin_t)
        wq = w_q_b[l]
        wq_perm = jnp.concatenate([wq[:, :, :nope].reshape(q_rank, H_mla * nope),
                                   wq[:, :, nope:].reshape(q_rank, H_mla * rope)], axis=1).astype(BF16)
        qa = matmul(qn, wq_perm, F32, name="q_proj")
        w_uk_t = jnp.transpose(w_uk[l], (1, 2, 0)).astype(BF16)
        w_uv_t = jnp.transpose(w_uv[l], (1, 0, 2)).astype(BF16)
        ql_p, qp_p = q_absorb(qa, 0, Tp, H_mla, nope, rope, kv_rank, w_uk_t, cos_t, sin_t, scale, BF16)
        ql_s, qp_s = q_absorb(qa, Tp, Ts, H_mla, nope, rope, kv_rank, w_uk_t, cos_t, sin_t, scale, F32)
        ob_p = attn_prompt(ql_p, qp_p, ckv_b, kpe_b, w_uv_t, Bp, Sp)
        ob_s = attn_sample(ql_s, qp_s, ckv_f, kpe_f, Tp, w_uv_t, cache_kv_latent, cache_k_rope,
                           page_table, l, Bs, Ss)
        o_b = jnp.concatenate([ob_p, ob_s], axis=0)

        mixin = gated_merge(o_a, o_b, w_branch[l].astype(BF16), proj, lay["gate_off"], D)
        pre = matmul_residual(mixin, w_out[l].astype(BF16), h, alpha)
        x1, x1b = layer_norm_rows(pre, ln_mix_g[l], ln_mix_b[l])

        s = router_scores(x1, w_router[l])
        idx, wts = _route(s, router_bias[l])
        bm = 384 if (T * TOP_K) // E >= 256 else 128
        slot_tok, slot_of, blk_expert, n_used = _dispatch_plan(idx, E, bm)
        x_pad = jnp.concatenate([x1b, jnp.zeros((1, D), BF16)], axis=0)
        x_sorted = x_pad[slot_tok]
        y_sorted = expert_ffn(x_sorted, blk_expert, n_used, w_exp_up, w_exp_down, l, bm)
        routed = jnp.sum(y_sorted[slot_of] * wts[:, :, None], axis=1)
        h = shared_ffn_final(x1b, x1, routed, w_shared_up[l].astype(BF16), w_shared_down[l].astype(BF16),
                             ln_ffn_g[l], ln_ffn_b[l], alpha)
        if l + 1 < L:
            hb = h.astype(BF16)

        st_p.append((ckv_f[:Tp].reshape(Bp, Sp, kv_rank), kpe_f[:Tp].reshape(Bp, Sp, rope), wkv_p, sh_p))
        st_s.append((ckv_f[Tp:].reshape(Bs, Ss, kv_rank), kpe_f[Tp:].reshape(Bs, Ss, rope), wkv_s, sh_s))

    y_p = h[:Tp].reshape(Bp, Sp, D)
    y_s = h[Tp:].reshape(Bs, Ss, D)
    stack = lambda sts, k: jnp.stack([s[k] for s in sts])
    return (y_p, y_s, stack(st_p, 0), stack(st_p, 1), stack(st_p, 2), stack(st_p, 3),
            stack(st_s, 0), stack(st_s, 1), stack(st_s, 2), stack(st_s, 3))
```

```python
import functools

import jax
import jax.numpy as jnp
from jax import lax
from jax.experimental import pallas as pl
from jax.experimental.pallas import tpu as pltpu

F32 = jnp.float32
BF16 = jnp.bfloat16
I32 = jnp.int32
U32 = jnp.uint32

TOP_K = 8
N_GROUPS = 8
TOPK_GROUPS = 4
ROUTED_SCALE = 2.5
LN_EPS = 1e-5
RMS_EPS = 1e-6
GN_EPS = 64e-5
ROPE_THETA = 10000.0
KK_NORM_FLOOR = 1e-12

LANES = 128
SUBLANES = 8
VMEM_LIMIT = 56 * 1024 * 1024
HEAD = 64
SCAN_UNROLL = 8
HI16 = 0xFFFF0000


def _cparams(*sem):
    return pltpu.CompilerParams(dimension_semantics=sem, vmem_limit_bytes=VMEM_LIMIT)


def _tile(n, want):
    t = min(n, want)
    while n % t:
        t -= 1
    return t


def _ln_math(x, g, b):
    mu = jnp.mean(x, axis=-1, keepdims=True)
    xc = x - mu
    var = jnp.mean(xc * xc, axis=-1, keepdims=True)
    return xc * lax.rsqrt(var + LN_EPS) * g + b


def _ln_kernel(x_ref, g_ref, b_ref, of_ref, ob_ref):
    y = _ln_math(x_ref[...], g_ref[...], b_ref[...])
    of_ref[...] = y
    ob_ref[...] = y.astype(BF16)


def _ln_pack_kernel(x_ref, g_ref, b_ref, of_ref, ob_ref, op_ref):
    y = _ln_math(x_ref[...], g_ref[...], b_ref[...])
    yb = y.astype(BF16)
    of_ref[...] = y
    ob_ref[...] = yb
    half = y.shape[1] // 2
    bits = pltpu.bitcast(yb.astype(F32), U32)
    op_ref[...] = (bits[:, half:] & jnp.uint32(HI16)) | (bits[:, :half] >> 16)


def layer_norm_rows(x, g, b, packed=False):
    T, D = x.shape
    tm = _tile(T, 256)
    row = pl.BlockSpec((tm, D), lambda i: (i, 0))
    vec = pl.BlockSpec((1, D), lambda i: (0, 0))
    out_specs = [row, row]
    out_shape = [jax.ShapeDtypeStruct((T, D), F32), jax.ShapeDtypeStruct((T, D), BF16)]
    if packed:
        out_specs.append(pl.BlockSpec((tm, D // 2), lambda i: (i, 0)))
        out_shape.append(jax.ShapeDtypeStruct((T, D // 2), U32))
    return pl.pallas_call(
        _ln_pack_kernel if packed else _ln_kernel,
        grid=(T // tm,),
        in_specs=[row, vec, vec],
        out_specs=out_specs,
        out_shape=out_shape,
        compiler_params=_cparams("parallel"),
        name="layer_norm_rows",
    )(x, g.reshape(1, D), b.reshape(1, D))


def _mm_kernel(x_ref, w_ref, o_ref):
    o_ref[...] = jnp.dot(x_ref[...], w_ref[...], preferred_element_type=F32).astype(o_ref.dtype)


def matmul(x, w, out_dtype, tm_want=1024, tn_want=512, name="matmul"):
    M, K = x.shape
    _, N = w.shape
    tm, tn = _tile(M, tm_want), _tile(N, tn_want)
    return pl.pallas_call(
        _mm_kernel,
        grid=(M // tm, N // tn),
        in_specs=[pl.BlockSpec((tm, K), lambda i, j: (i, 0)),
                  pl.BlockSpec((K, tn), lambda i, j: (0, j))],
        out_specs=pl.BlockSpec((tm, tn), lambda i, j: (i, j)),
        out_shape=jax.ShapeDtypeStruct((M, N), out_dtype),
        compiler_params=_cparams("parallel", "arbitrary"),
        name=name,
    )(x, w)


def _rwkv_prep_kernel(misc_ref, misc_prev_ref, sh_ref, mu_ref, w0_ref, wdec_ref, a0_ref, wicl_ref, wg_ref,
                      w_out, a_out, g_out, *, n_gate, n_dec, n_icl):
    p = misc_ref[...]
    prev_row = jnp.where(pl.program_id(1) == 0, sh_ref[0], misc_prev_ref[SUBLANES - 1:SUBLANES, :])
    rolled = pltpu.roll(p, 1, axis=0)
    rid = lax.broadcasted_iota(I32, (p.shape[0], 1), 0)
    p_prev = jnp.where(rid == 0, prev_row, rolled)
    ps = p + (p_prev - p) * mu_ref[...]
    xg = ps[:, :n_gate]
    xw = ps[:, n_gate:n_gate + n_dec]
    xa = ps[:, n_gate + n_dec:n_gate + n_dec + n_icl]

    dec_in = w0_ref[...] + jnp.dot(jnp.tanh(xw).astype(BF16), wdec_ref[...], preferred_element_type=F32)
    z = -dec_in
    softplus = jnp.maximum(z, 0.0) + jnp.log1p(jnp.exp(-jnp.abs(z)))
    w_log = -softplus - 0.5
    w_out[...] = jnp.exp(-jnp.exp(w_log))
    a_out[...] = jax.nn.sigmoid(a0_ref[...] + jnp.dot(xa.astype(BF16), wicl_ref[...], preferred_element_type=F32))
    g_out[...] = jnp.dot(jax.nn.sigmoid(xg).astype(BF16), wg_ref[...], preferred_element_type=F32)


def rwkv_prep(proj, row_off, B, S, lay, shift_misc, prm):
    W = lay["W"]
    Tc = _tile(S, 256)
    nblk = S // Tc
    base = row_off // Tc
    base8 = row_off // SUBLANES
    per8 = Tc // SUBLANES
    mw = lay["misc_w"]
    mi = lay["misc_off"] // mw
    full = lambda shape: pl.BlockSpec(shape, lambda b, i: (0,) * len(shape))
    out_spec = pl.BlockSpec((Tc, W), lambda b, i: (b * nblk + i, 0))
    out_sds = jax.ShapeDtypeStruct((B * S, W), F32)
    kern = functools.partial(_rwkv_prep_kernel, n_gate=lay["n_gate"], n_dec=lay["n_dec"], n_icl=lay["n_icl"])
    return pl.pallas_call(
        kern,
        grid=(B, nblk),
        in_specs=[
            pl.BlockSpec((Tc, mw), lambda b, i: (base + b * nblk + i, mi)),
            pl.BlockSpec((SUBLANES, mw), lambda b, i: (jnp.maximum(base8 + (b * nblk + i) * per8 - 1, 0), mi)),
            pl.BlockSpec((1, 1, mw), lambda b, i: (b, 0, 0)),
            full((1, mw)), full((1, W)), full((lay["n_dec"], W)), full((1, W)), full((lay["n_icl"], W)),
            full((lay["n_gate"], W)),
        ],
        out_specs=[out_spec] * 3,
        out_shape=[out_sds] * 3,
        compiler_params=_cparams("parallel", "arbitrary"),
        name="rwkv_prep",
    )(proj, proj, shift_misc, prm["mu_misc"], prm["w0"], prm["w_dec"], prm["a0"], prm["w_icl"], prm["w_g"])


def _rwkv_scan_kernel(r_ref, k_ref, v_ref, a_ref, w_ref, prev0_ref, mu_ref, prm_ref, s0_ref,
                      o_ref, s_ref, prev_scr, vs_scr, y_scr, *, Tc):
    tc = pl.program_id(1)

    @pl.when(tc == 0)
    def _():
        s_ref[...] = s0_ref[...]
        prev_scr[...] = prev0_ref[...]

    def step(t, carry):
        pr, pk, pv = r_ref[t], k_ref[t], v_ref[t]
        rs = pr + (prev_scr[0] - pr) * mu_ref[0]
        ks = pk + (prev_scr[1] - pk) * mu_ref[1]
        vs = pv + (prev_scr[2] - pv) * mu_ref[2]
        prev_scr[0] = pr
        prev_scr[1] = pk
        prev_scr[2] = pv
        a = a_ref[t]
        w = w_ref[t]
        kk = ks * prm_ref[0]
        ss = jnp.sum(kk * kk, axis=0, keepdims=True)
        kk = kk / jnp.maximum(jnp.sqrt(ss), KK_NORM_FLOOR)
        k2 = ks * (1.0 + (a - 1.0) * prm_ref[1])
        b = kk * a
        kkn = -kk
        vs_scr[...] = vs

        def rows(vg, c):
            for u in range(SCAN_UNROLL):
                vi = vg * SCAN_UNROLL + u
                sv = s_ref[vi]
                sa = jnp.sum(sv * kkn, axis=0, keepdims=True)
                sv = sv * w + sa * b + vs_scr[pl.ds(vi, 1), :] * k2
                s_ref[vi] = sv
                y_scr[pl.ds(vi, 1), :] = jnp.sum(sv * rs, axis=0, keepdims=True)
            return c

        lax.fori_loop(0, HEAD // SCAN_UNROLL, rows, 0)
        y = y_scr[...]
        y_mu = jnp.mean(y, axis=0, keepdims=True)
        yc = y - y_mu
        y_var = jnp.mean(yc * yc, axis=0, keepdims=True)
        yn = yc * lax.rsqrt(y_var + GN_EPS) * prm_ref[3] + prm_ref[4]
        bonus = jnp.sum(rs * k2 * prm_ref[2], axis=0, keepdims=True) * vs
        o_ref[t] = yn + bonus
        return carry

    lax.fori_loop(0, Tc, step, 0)


def rwkv_scan(r, k, v, a, w, prev0, mu, prm, s0):
    S, _, N = r.shape
    Tc = _tile(S, 32)
    seq = pl.BlockSpec((Tc, HEAD, LANES), lambda g, t: (t, 0, g))
    st = pl.BlockSpec((HEAD, HEAD, LANES), lambda g, t: (0, 0, g))
    return pl.pallas_call(
        functools.partial(_rwkv_scan_kernel, Tc=Tc),
        grid=(N // LANES, S // Tc),
        in_specs=[seq] * 5 + [pl.BlockSpec((3, HEAD, LANES), lambda g, t: (0, 0, g)),
                              pl.BlockSpec((3, HEAD, LANES), lambda g, t: (0, 0, 0)),
                              pl.BlockSpec((5, HEAD, LANES), lambda g, t: (0, 0, 0)), st],
        out_specs=[seq, st],
        out_shape=[jax.ShapeDtypeStruct((S, HEAD, N), F32), jax.ShapeDtypeStruct((HEAD, HEAD, N), F32)],
        scratch_shapes=[pltpu.VMEM((3, HEAD, LANES), F32), pltpu.VMEM((HEAD, LANES), F32),
                        pltpu.VMEM((HEAD, LANES), F32)],
        compiler_params=_cparams("parallel", "arbitrary"),
        name="rwkv_scan",
    )(r, k, v, a, w, prev0, mu, prm, s0)


def _gate_kernel(o_ref, g_ref, out_ref):
    out_ref[...] = (o_ref[...] * g_ref[...]).astype(out_ref.dtype)


def rwkv_gate(o_pre, g):
    T, W = o_pre.shape
    tm = _tile(T, 512)
    row = pl.BlockSpec((tm, W), lambda i: (i, 0))
    return pl.pallas_call(
        _gate_kernel, grid=(T // tm,), in_specs=[row, row], out_specs=row,
        out_shape=jax.ShapeDtypeStruct((T, W), BF16),
        compiler_params=_cparams("parallel"), name="rwkv_gate",
    )(o_pre, g)


def _to_lanes(x, B, S, H):
    return x.reshape(B, S, H, HEAD).transpose(1, 3, 0, 2).reshape(S, HEAD, B * H)


def _from_lanes(x, B, S, H):
    return x.reshape(S, HEAD, B, H).transpose(2, 0, 3, 1).reshape(B * S, H * HEAD)


def _head_tile(p, H):
    return jnp.tile(p.reshape(H, HEAD).T, (1, LANES // H))


def _swap_halves(x, width):
    n = x.shape[1]
    half = width // 2
    lane = lax.broadcasted_iota(I32, x.shape, 1)
    fwd = pltpu.roll(x, half, axis=1)
    bwd = pltpu.roll(x, n - half, axis=1)
    return jnp.where(lane % width < half, bwd, fwd)


def _mla_prep_kernel(pq_ref, ckv_ref, misc_ref, qg_ref, kvg_ref, cos_ref, sin_ref,
                     qn_ref, ckv_f_ref, ckv_b_ref, kpe_f_ref, kpe_b_ref, *, pe_off, rope):
    def rms(x, g):
        return x * lax.rsqrt(jnp.mean(x * x, axis=-1, keepdims=True) + RMS_EPS) * g

    qn_ref[...] = rms(pq_ref[...], qg_ref[...]).astype(BF16)
    ckv = rms(ckv_ref[...], kvg_ref[...])
    ckv_f_ref[...] = ckv
    ckv_b_ref[...] = ckv.astype(BF16)
    blk = misc_ref[:, pe_off - pe_off % LANES: pe_off - pe_off % LANES + LANES]
    roped = blk * cos_ref[...] + _swap_halves(blk, rope) * sin_ref[...]
    kpe = roped[:, pe_off % LANES: pe_off % LANES + rope]
    kpe_f_ref[...] = kpe
    kpe_b_ref[...] = kpe.astype(BF16)


def mla_prep(proj, lay, q_norm_g, kv_norm_g, cos_t, sin_t):
    T = proj.shape[0]
    QR, KR, rope, mw = lay["q_rank"], lay["kv_rank"], lay["rope"], lay["misc_w"]
    tm = _tile(T, 256)
    row = lambda wd, col: pl.BlockSpec((tm, wd), lambda i: (i, col))
    vec = lambda wd: pl.BlockSpec((1, wd), lambda i: (0, 0))
    kern = functools.partial(_mla_prep_kernel, pe_off=lay["pe_in_misc"], rope=rope)
    return pl.pallas_call(
        kern,
        grid=(T // tm,),
        in_specs=[row(QR, lay["q_off"] // QR), row(KR, lay["ckv_off"] // KR), row(mw, lay["misc_off"] // mw),
                  vec(QR), vec(KR), row(LANES, 0), row(LANES, 0)],
        out_specs=[row(QR, 0), row(KR, 0), row(KR, 0), row(rope, 0), row(rope, 0)],
        out_shape=[jax.ShapeDtypeStruct((T, QR), BF16), jax.ShapeDtypeStruct((T, KR), F32),
                   jax.ShapeDtypeStruct((T, KR), BF16), jax.ShapeDtypeStruct((T, rope), F32),
                   jax.ShapeDtypeStruct((T, rope), BF16)],
        compiler_params=_cparams("parallel"),
        name="mla_prep",
    )(proj, proj, proj, q_norm_g.reshape(1, QR), kv_norm_g.reshape(1, KR), cos_t, sin_t)


def _q_absorb_kernel(nope_ref, pe_ref, wuk_ref, cos_ref, sin_ref, ql_ref, qp_ref, *, nope, rope, scale):
    hp = LANES // rope
    for h in range(hp):
        qn = nope_ref[:, h * nope:(h + 1) * nope].astype(BF16)
        ql = jnp.dot(qn, wuk_ref[h], preferred_element_type=F32) * scale
        ql_ref[h] = ql.astype(ql_ref.dtype)
    pe = pe_ref[...]
    roped = (pe * cos_ref[...] + _swap_halves(pe, rope) * sin_ref[...]) * scale
    for h in range(hp):
        qp_ref[h] = roped[:, h * rope:(h + 1) * rope].astype(qp_ref.dtype)


def q_absorb(qa, row_off, Tg, H, nope, rope, kv_rank, w_uk_t, cos_t, sin_t, scale, out_dtype):
    hp = LANES // rope
    tm = _tile(Tg, 256)
    base = row_off // tm
    pe_col0 = (H * nope) // LANES
    kern = functools.partial(_q_absorb_kernel, nope=nope, rope=rope, scale=scale)
    return pl.pallas_call(
        kern,
        grid=(Tg // tm, H // hp),
        in_specs=[pl.BlockSpec((tm, hp * nope), lambda i, j: (base + i, j)),
                  pl.BlockSpec((tm, LANES), lambda i, j: (base + i, pe_col0 + j)),
                  pl.BlockSpec((hp, nope, kv_rank), lambda i, j: (j, 0, 0)),
                  pl.BlockSpec((tm, LANES), lambda i, j: (base + i, 0)),
                  pl.BlockSpec((tm, LANES), lambda i, j: (base + i, 0))],
        out_specs=[pl.BlockSpec((hp, tm, kv_rank), lambda i, j: (j, i, 0)),
                   pl.BlockSpec((hp, tm, rope), lambda i, j: (j, i, 0))],
        out_shape=[jax.ShapeDtypeStruct((H, Tg, kv_rank), out_dtype),
                   jax.ShapeDtypeStruct((H, Tg, rope), out_dtype)],
        compiler_params=_cparams("parallel", "arbitrary"),
        name="q_absorb",
    )(qa, qa, w_uk_t, cos_t, sin_t)


_NT = (((1,), (1,)), ((), ()))


def _attn_prompt_kernel(ql_ref, qp_ref, ckv_ref, kpe_ref, wuv_ref, o_ref, m_ref, l_ref, acc_ref, *, H, TQ, CK, vdim):
    i = pl.program_id(1)
    R = H * TQ
    q = ql_ref[...].reshape(R, ql_ref.shape[-1])
    qp = qp_ref[...].reshape(R, qp_ref.shape[-1])
    m_ref[...] = jnp.full(m_ref.shape, -jnp.inf, F32)
    l_ref[...] = jnp.zeros(l_ref.shape, F32)
    acc_ref[...] = jnp.zeros(acc_ref.shape, F32)
    q_pos = i * TQ + lax.broadcasted_iota(I32, (R, 1), 0) % TQ
    n_chunks = ((i + 1) * TQ + CK - 1) // CK

    def chunk(j, carry):
        ks = pl.ds(pl.multiple_of(j * CK, CK), CK)
        kc = ckv_ref[ks, :]
        s = lax.dot_general(q, kc, _NT, preferred_element_type=F32)
        s = s + lax.dot_general(qp, kpe_ref[ks, :], _NT, preferred_element_type=F32)
        key_pos = j * CK + lax.broadcasted_iota(I32, (1, CK), 1)
        s = jnp.where(key_pos <= q_pos, s, -jnp.inf)
        m_old = m_ref[...]
        m_new = jnp.maximum(m_old, jnp.max(s, axis=-1, keepdims=True))
        corr = jnp.exp(m_old - m_new)
        p = jnp.exp(s - m_new)
        l_ref[...] = l_ref[...] * corr + jnp.sum(p, axis=-1, keepdims=True)
        acc_ref[...] = acc_ref[...] * corr + jnp.dot(p.astype(BF16), kc, preferred_element_type=F32)
        m_ref[...] = m_new
        return carry

    lax.fori_loop(0, n_chunks, chunk, 0)
    o = (acc_ref[...] / l_ref[...]).astype(BF16)
    for h in range(H):
        o_ref[:, h * vdim:(h + 1) * vdim] = jnp.dot(
            o[h * TQ:(h + 1) * TQ, :], wuv_ref[h], preferred_element_type=F32).astype(o_ref.dtype)


def attn_prompt(q_lat, q_pe, ckv_b, kpe_b, w_uv_t, B, S):
    H, _, KR = q_lat.shape
    rope = q_pe.shape[-1]
    vdim = w_uv_t.shape[-1]
    TQ = _tile(S, 64)
    CK = _tile(S, 512)
    nq = S // TQ
    kern = functools.partial(_attn_prompt_kernel, H=H, TQ=TQ, CK=CK, vdim=vdim)
    return pl.pallas_call(
        kern,
        grid=(B, nq),
        in_specs=[pl.BlockSpec((H, TQ, KR), lambda b, i: (0, b * nq + i, 0)),
                  pl.BlockSpec((H, TQ, rope), lambda b, i: (0, b * nq + i, 0)),
                  pl.BlockSpec((S, KR), lambda b, i: (b, 0)),
                  pl.BlockSpec((S, rope), lambda b, i: (b, 0)),
                  pl.BlockSpec((H, KR, vdim), lambda b, i: (0, 0, 0))],
        out_specs=pl.BlockSpec((TQ, H * vdim), lambda b, i: (b * nq + i, 0)),
        out_shape=jax.ShapeDtypeStruct((B * S, H * vdim), BF16),
        scratch_shapes=[pltpu.VMEM((H * TQ, 1), F32), pltpu.VMEM((H * TQ, 1), F32),
                        pltpu.VMEM((H * TQ, KR), F32)],
        compiler_params=_cparams("parallel", "arbitrary"),
        name="attn_prompt",
    )(q_lat, q_pe, ckv_b, kpe_b, w_uv_t)


def _attn_sample_kernel(pt_ref, ql_ref, qp_ref, ckv_ref, kpe_ref, wuv_ref, *rest, H, TS, G, vdim):
    kv_refs = rest[:G]
    kp_refs = rest[G:2 * G]
    o_ref, m_ref, l_ref, acc_ref = rest[2 * G:]
    g = pl.program_id(1)
    R = H * TS
    q32 = ql_ref[...].reshape(R, ql_ref.shape[-1])
    qp32 = qp_ref[...].reshape(R, qp_ref.shape[-1])
    q = q32.astype(BF16)
    qp = qp32.astype(BF16)

    @pl.when(g == 0)
    def _():
        ck = ckv_ref[...]
        s = lax.dot_general(q32, ck, _NT, preferred_element_type=F32)
        s = s + lax.dot_general(qp32, kpe_ref[...], _NT, preferred_element_type=F32)
        t_row = lax.broadcasted_iota(I32, (R, 1), 0) % TS
        t_key = lax.broadcasted_iota(I32, (1, TS), 1)
        s = jnp.where(t_key <= t_row, s, -jnp.inf)
        m0 = jnp.max(s, axis=-1, keepdims=True)
        p = jnp.exp(s - m0)
        m_ref[...] = m0
        l_ref[...] = jnp.sum(p, axis=-1, keepdims=True)
        acc_ref[...] = jnp.dot(p, ck, preferred_element_type=F32)

    pages = [kv_refs[j][0, 0].astype(BF16) for j in range(G)]
    s_parts = []
    for j in range(G):
        sj = lax.dot_general(q, pages[j], _NT, preferred_element_type=F32)
        sj = sj + jnp.dot(qp, kp_refs[j][0, 0].astype(BF16), preferred_element_type=F32)
        s_parts.append(sj)
    s = jnp.concatenate(s_parts, axis=1) if G > 1 else s_parts[0]
    m_old = m_ref[...]
    m_new = jnp.maximum(m_old, jnp.max(s, axis=-1, keepdims=True))
    corr = jnp.exp(m_old - m_new)
    p = jnp.exp(s - m_new)
    l_ref[...] = l_ref[...] * corr + jnp.sum(p, axis=-1, keepdims=True)
    pb = p.astype(BF16)
    P = pages[0].shape[0]
    pv = jnp.dot(pb[:, :P], pages[0], preferred_element_type=F32)
    for j in range(1, G):
        pv = pv + jnp.dot(pb[:, j * P:(j + 1) * P], pages[j], preferred_element_type=F32)
    acc_ref[...] = acc_ref[...] * corr + pv
    m_ref[...] = m_new

    @pl.when(g == pl.num_programs(1) - 1)
    def _():
        o = (acc_ref[...] / l_ref[...]).astype(BF16)
        for h in range(H):
            o_ref[:, h * vdim:(h + 1) * vdim] = jnp.dot(
                o[h * TS:(h + 1) * TS, :], wuv_ref[h], preferred_element_type=F32).astype(o_ref.dtype)


def attn_sample(q_lat, q_pe, ckv_f, kpe_f, row_off, w_uv_t, cache_kv, cache_kpe_t, page_table, layer, DB, TS):
    H, _, KR = q_lat.shape
    rope = q_pe.shape[-1]
    vdim = w_uv_t.shape[-1]
    n_pages = page_table.shape[1]
    P = cache_kv.shape[2]
    G = _tile(n_pages, 16)
    base = row_off // TS

    def page_spec(shape, j):
        return pl.BlockSpec((1, 1) + shape, lambda b, g, pt: (layer, pt[b, g * G + j], 0, 0))

    kern = functools.partial(_attn_sample_kernel, H=H, TS=TS, G=G, vdim=vdim)
    grid_spec = pltpu.PrefetchScalarGridSpec(
        num_scalar_prefetch=1,
        grid=(DB, n_pages // G),
        in_specs=[pl.BlockSpec((H, TS, KR), lambda b, g, pt: (0, b, 0)),
                  pl.BlockSpec((H, TS, rope), lambda b, g, pt: (0, b, 0)),
                  pl.BlockSpec((TS, KR), lambda b, g, pt: (base + b, 0)),
                  pl.BlockSpec((TS, rope), lambda b, g, pt: (base + b, 0)),
                  pl.BlockSpec((H, KR, vdim), lambda b, g, pt: (0, 0, 0))]
                 + [page_spec((P, KR), j) for j in range(G)] + [page_spec((rope, P), j) for j in range(G)],
        out_specs=pl.BlockSpec((TS, H * vdim), lambda b, g, pt: (b, 0)),
        scratch_shapes=[pltpu.VMEM((H * TS, 1), F32), pltpu.VMEM((H * TS, 1), F32),
                        pltpu.VMEM((H * TS, KR), F32)],
    )
    return pl.pallas_call(
        kern,
        grid_spec=grid_spec,
        out_shape=jax.ShapeDtypeStruct((DB * TS, H * vdim), BF16),
        compiler_params=_cparams("parallel", "arbitrary"),
        name="attn_sample",
    )(page_table, q_lat, q_pe, ckv_f, kpe_f, w_uv_t, *([cache_kv] * G), *([cache_kpe_t] * G))


def _merge_kernel(oa_ref, ob_ref, wa_ref, wb_ref, ga_ref, gb_ref, o_ref):
    a = jnp.dot(oa_ref[...], wa_ref[...], preferred_element_type=F32)
    b = jnp.dot(ob_ref[...], wb_ref[...], preferred_element_type=F32)
    o_ref[...] = (jax.nn.sigmoid(ga_ref[...]) * a + jax.nn.sigmoid(gb_ref[...]) * b).astype(o_ref.dtype)


def gated_merge(o_a, o_b, w_branch_b, proj, gate_off, D):
    T, Wb = o_a.shape
    tm, tn = _tile(T, 1024), _tile(D, 512)
    g0 = gate_off // tn
    g1 = (gate_off + D) // tn
    return pl.pallas_call(
        _merge_kernel,
        grid=(T // tm, D // tn),
        in_specs=[pl.BlockSpec((tm, Wb), lambda i, j: (i, 0)),
                  pl.BlockSpec((tm, Wb), lambda i, j: (i, 0)),
                  pl.BlockSpec((None, Wb, tn), lambda i, j: (0, 0, j)),
                  pl.BlockSpec((None, Wb, tn), lambda i, j: (1, 0, j)),
                  pl.BlockSpec((tm, tn), lambda i, j: (i, g0 + j)),
                  pl.BlockSpec((tm, tn), lambda i, j: (i, g1 + j))],
        out_specs=pl.BlockSpec((tm, tn), lambda i, j: (i, j)),
        out_shape=jax.ShapeDtypeStruct((T, D), BF16),
        compiler_params=_cparams("parallel", "arbitrary"),
        name="gated_merge",
    )(o_a, o_b, w_branch_b, w_branch_b, proj, proj)


def _mm_resid_kernel(x_ref, w_ref, h_ref, o_ref, *, alpha):
    o_ref[...] = alpha * h_ref[...] + jnp.dot(x_ref[...], w_ref[...], preferred_element_type=F32)


def matmul_residual(x, w, h, alpha):
    M, K = x.shape
    N = w.shape[1]
    tm, tn = _tile(M, 1024), _tile(N, 512)
    return pl.pallas_call(
        functools.partial(_mm_resid_kernel, alpha=alpha),
        grid=(M // tm, N // tn),
        in_specs=[pl.BlockSpec((tm, K), lambda i, j: (i, 0)),
                  pl.BlockSpec((K, tn), lambda i, j: (0, j)),
                  pl.BlockSpec((tm, tn), lambda i, j: (i, j))],
        out_specs=pl.BlockSpec((tm, tn), lambda i, j: (i, j)),
        out_shape=jax.ShapeDtypeStruct((M, N), F32),
        compiler_params=_cparams("parallel", "arbitrary"),
        name="matmul_residual",
    )(x, w, h)


def _first_index(hit, lane, size):
    return jnp.min(jnp.where(hit, lane, size), axis=1, keepdims=True)


def _route_kernel(x_ref, wh_ref, wl_ref, bias_ref, tri_ref, idx_ref, wts_ref, rank_ref, cnt_ref, run_ref,
                  *, n_groups, topk_groups, top_k):
    @pl.when(pl.program_id(0) == 0)
    def _():
        run_ref[...] = jnp.zeros(run_ref.shape, F32)

    x = x_ref[...]
    xh = x.astype(BF16)
    xl = (x - xh.astype(F32)).astype(BF16)
    wh = wh_ref[...]
    logits = jnp.dot(xh, wh, preferred_element_type=F32)
    logits = logits + jnp.dot(xl, wh, preferred_element_type=F32)
    logits = logits + jnp.dot(xh, wl_ref[...], preferred_element_type=F32)
    s = jax.nn.sigmoid(logits)
    sc = s + bias_ref[...]
    tm, E = s.shape
    gsz = E // n_groups
    neg = -jnp.inf
    lane = lax.broadcasted_iota(I32, (tm, E), 1)
    gid = lane // gsz
    lane_s = lax.broadcasted_iota(I32, (tm, LANES), 1)

    grp = jnp.full((tm, LANES), neg, F32)
    for g in range(n_groups):
        m = jnp.where(gid == g, sc, neg)
        t1 = jnp.max(m, axis=1, keepdims=True)
        i1 = _first_index(m == t1, lane, E)
        t2 = jnp.max(jnp.where(lane == i1, neg, m), axis=1, keepdims=True)
        grp = jnp.where(lane_s == g, t1 + t2, grp)

    allowed = jnp.zeros((tm, E), I32)
    for _ in range(topk_groups):
        mx = jnp.max(grp, axis=1, keepdims=True)
        gi = _first_index(grp == mx, lane_s, LANES)
        grp = jnp.where(lane_s == gi, neg, grp)
        allowed = jnp.where(gid == gi, 1, allowed)
    masked = jnp.where(allowed > 0, sc, neg)

    onehot = jnp.zeros((tm, E), F32)
    idx_t = jnp.zeros((tm, LANES), I32)
    w_t = jnp.zeros((tm, LANES), F32)
    picks = []
    for k in range(top_k):
        mx = jnp.max(masked, axis=1, keepdims=True)
        ei = _first_index(masked == mx, lane, E)
        hit = lane == ei
        masked = jnp.where(hit, neg, masked)
        onehot = jnp.where(hit, 1.0, onehot)
        sv = jnp.sum(jnp.where(hit, s, 0.0), axis=1, keepdims=True)
        idx_t = jnp.where(lane_s == k, ei, idx_t)
        w_t = jnp.where(lane_s == k, sv, w_t)
        picks.append(hit)
    wts = w_t / jnp.sum(w_t, axis=1, keepdims=True) * ROUTED_SCALE

    before = jnp.dot(tri_ref[...], onehot.astype(BF16), preferred_element_type=F32) + run_ref[...]
    rank_t = jnp.zeros((tm, LANES), F32)
    for k in range(top_k):
        rk = jnp.sum(jnp.where(picks[k], before, 0.0), axis=1, keepdims=True)
        rank_t = jnp.where(lane_s == k, rk, rank_t)
    run_ref[...] += jnp.sum(onehot, axis=0, keepdims=True)

    idx_ref[...] = idx_t[:, :top_k]
    wts_ref[...] = wts[:, :top_k]
    rank_ref[...] = rank_t[:, :top_k].astype(I32)
    cnt_ref[...] = run_ref[...]


def route(x, w_router, router_bias):
    T, D = x.shape
    E = w_router.shape[1]
    wh = w_router.astype(BF16)
    wl = (w_router - wh.astype(F32)).astype(BF16)
    tm = _tile(T, 256)
    tri = (jnp.arange(tm)[:, None] > jnp.arange(tm)[None, :]).astype(BF16)
    kern = functools.partial(_route_kernel, n_groups=N_GROUPS, topk_groups=TOPK_GROUPS, top_k=TOP_K)
    small = pl.BlockSpec((tm, TOP_K), lambda i: (i, 0))
    return pl.pallas_call(
        kern,
        grid=(T // tm,),
        in_specs=[pl.BlockSpec((tm, D), lambda i: (i, 0)),
                  pl.BlockSpec((D, E), lambda i: (0, 0)),
                  pl.BlockSpec((D, E), lambda i: (0, 0)),
                  pl.BlockSpec((1, E), lambda i: (0, 0)),
                  pl.BlockSpec((tm, tm), lambda i: (0, 0))],
        out_specs=[small, small, small, pl.BlockSpec((1, E), lambda i: (0, 0))],
        out_shape=[jax.ShapeDtypeStruct((T, TOP_K), I32), jax.ShapeDtypeStruct((T, TOP_K), F32),
                   jax.ShapeDtypeStruct((T, TOP_K), I32), jax.ShapeDtypeStruct((1, E), F32)],
        scratch_shapes=[pltpu.VMEM((1, E), F32)],
        compiler_params=_cparams("arbitrary"),
        name="route",
    )(x, wh, wl, router_bias.reshape(1, E).astype(F32), tri)


def _slot_kernel(idx_ref, rank_ref, base_ref, dest_ref):
    idx = idx_ref[...]
    tm, K = idx.shape
    E = base_ref.shape[1]
    lane = lax.broadcasted_iota(I32, (tm, E), 1)
    lane_s = lax.broadcasted_iota(I32, (tm, LANES), 1)
    base = base_ref[...]
    out = jnp.zeros((tm, LANES), F32)
    for k in range(K):
        bk = jnp.sum(jnp.where(lane == idx[:, k:k + 1], base, 0.0), axis=1, keepdims=True)
        out = jnp.where(lane_s == k, bk, out)
    dest_ref[...] = out[:, :K].astype(I32) + rank_ref[...]


def slot_of_assignment(idx, rank, base):
    T, K = idx.shape
    E = base.shape[1]
    tm = _tile(T, 512)
    small = pl.BlockSpec((tm, K), lambda i: (i, 0))
    return pl.pallas_call(
        _slot_kernel, grid=(T // tm,),
        in_specs=[small, small, pl.BlockSpec((1, E), lambda i: (0, 0))],
        out_specs=small, out_shape=jax.ShapeDtypeStruct((T, K), I32),
        compiler_params=_cparams("parallel"), name="slot_of_assignment",
    )(idx, rank, base)


def _expert_kernel(be_ref, nu_ref, x_ref, wg_ref, wu_ref, wd_ref, o_ref):
    blk = pl.program_id(0)
    j = pl.program_id(1)

    @pl.when(blk < nu_ref[0])
    def _():
        xu = x_ref[...]
        half = xu.shape[1]
        x_lo = pltpu.bitcast(xu << 16, F32).astype(BF16)
        x_hi = pltpu.bitcast(xu & jnp.uint32(HI16), F32).astype(BF16)

        def up(w_ref):
            w = w_ref[0, 0].astype(BF16)
            return (jnp.dot(x_lo, w[:half], preferred_element_type=F32)
                    + jnp.dot(x_hi, w[half:], preferred_element_type=F32))

        hg = up(wg_ref)
        hu = up(wu_ref)
        act = (hg * jax.nn.sigmoid(hg) * hu).astype(BF16)
        part = jnp.dot(act, wd_ref[0, 0].astype(BF16), preferred_element_type=F32)

        @pl.when(j == 0)
        def _():
            o_ref[...] = part

        @pl.when(j > 0)
        def _():
            o_ref[...] += part

    @pl.when(jnp.logical_and(blk >= nu_ref[0], j == 0))
    def _():
        o_ref[...] = jnp.zeros(o_ref.shape, F32)


def expert_ffn(x_sorted, blk_expert, n_used, w_exp_up, w_exp_down, layer, bm):
    n_slots, Dh = x_sorted.shape
    D = 2 * Dh
    DE = w_exp_down.shape[2]
    J = 2 if DE % (2 * LANES) == 0 else 1
    dh = DE // J
    n_blocks = n_slots // bm

    def live(blk, nu):
        return jnp.minimum(blk, nu[0] - 1)

    def jj(blk, j, nu):
        return jnp.where(blk < nu[0], j, J - 1)

    grid_spec = pltpu.PrefetchScalarGridSpec(
        num_scalar_prefetch=2,
        grid=(n_blocks, J),
        in_specs=[pl.BlockSpec((bm, Dh), lambda b, j, be, nu: (live(b, nu), 0)),
                  pl.BlockSpec((1, 1, D, dh), lambda b, j, be, nu: (layer, be[live(b, nu)], 0, jj(b, j, nu))),
                  pl.BlockSpec((1, 1, D, dh), lambda b, j, be, nu: (layer, be[live(b, nu)], 0, J + jj(b, j, nu))),
                  pl.BlockSpec((1, 1, dh, D), lambda b, j, be, nu: (layer, be[live(b, nu)], jj(b, j, nu), 0))],
        out_specs=pl.BlockSpec((bm, D), lambda b, j, be, nu: (b, 0)),
    )
    return pl.pallas_call(
        _expert_kernel,
        grid_spec=grid_spec,
        out_shape=jax.ShapeDtypeStruct((n_slots, D), F32),
        compiler_params=_cparams("arbitrary", "arbitrary"),
        name="expert_ffn",
    )(blk_expert, n_used, x_sorted, w_exp_up, w_exp_up, w_exp_down)


def _combine_kernel(y_ref, w_ref, o_ref):
    w = w_ref[...]
    acc = y_ref[:, 0, :] * w[:, 0:1]
    for k in range(1, w.shape[1]):
        acc = acc + y_ref[:, k, :] * w[:, k:k + 1]
    o_ref[...] = acc


def combine(y_tok, wts):
    T, K, D = y_tok.shape
    tm = _tile(T, 64)
    return pl.pallas_call(
        _combine_kernel, grid=(T // tm,),
        in_specs=[pl.BlockSpec((tm, K, D), lambda i: (i, 0, 0)), pl.BlockSpec((tm, K), lambda i: (i, 0))],
        out_specs=pl.BlockSpec((tm, D), lambda i: (i, 0)),
        out_shape=jax.ShapeDtypeStruct((T, D), F32),
        compiler_params=_cparams("parallel"), name="combine",
    )(y_tok, wts)


def _shared_final_kernel(xb_ref, x_ref, routed_ref, wu_ref, wd_ref, g_ref, b_ref, o_ref, *, alpha, DE):
    hs = jnp.dot(xb_ref[...], wu_ref[...], preferred_element_type=F32)
    hg, hu = hs[:, :DE], hs[:, DE:]
    act = (hg * jax.nn.sigmoid(hg) * hu).astype(BF16)
    shared = jnp.dot(act, wd_ref[...], preferred_element_type=F32)
    pre = alpha * x_ref[...] + (routed_ref[...] + shared)
    o_ref[...] = _ln_math(pre, g_ref[...], b_ref[...])


def shared_ffn_final(xb, x, routed, w_up_b, w_down_b, g, b, alpha):
    T, D = x.shape
    DE = w_down_b.shape[0]
    tm = _tile(T, 128)
    row = lambda: pl.BlockSpec((tm, D), lambda i: (i, 0))
    vec = pl.BlockSpec((1, D), lambda i: (0, 0))
    return pl.pallas_call(
        functools.partial(_shared_final_kernel, alpha=alpha, DE=DE),
        grid=(T // tm,),
        in_specs=[row(), row(), row(),
                  pl.BlockSpec((D, 2 * DE), lambda i: (0, 0)),
                  pl.BlockSpec((DE, D), lambda i: (0, 0)), vec, vec],
        out_specs=row(),
        out_shape=jax.ShapeDtypeStruct((T, D), F32),
        compiler_params=_cparams("parallel"),
        name="shared_ffn_final",
    )(xb, x, routed, w_up_b, w_down_b, g.reshape(1, D), b.reshape(1, D))


def _block_tables(counts, dest, T, K, E, bm):
    M = T * K
    n_blocks = -(-M // bm) + E
    n_slots = n_blocks * bm
    cnt = counts.reshape(E).astype(I32)
    blocks_per_e = (cnt + bm - 1) // bm
    blk_end = jnp.cumsum(blocks_per_e)
    blk_expert = jnp.minimum(jnp.searchsorted(blk_end, jnp.arange(n_blocks), side="right"), E - 1).astype(I32)
    n_used = blk_end[-1:].astype(I32)
    slot_tok = (jnp.arange(n_slots, dtype=I32) % T).at[dest.reshape(M)].set(jnp.arange(M, dtype=I32) // K)
    return slot_tok, blk_expert, n_used


def _in_layout(W, n_dec, n_icl, n_gate, q_rank, kv_rank, rope, D):
    misc_w = -(-(n_gate + n_dec + n_icl + rope) // LANES) * LANES
    lay = dict(W=W, n_dec=n_dec, n_icl=n_icl, n_gate=n_gate, q_rank=q_rank, kv_rank=kv_rank, rope=rope,
               misc_w=misc_w)
    off = 3 * W
    lay["gate_off"] = off
    off += 2 * D
    lay["q_off"] = off
    off += q_rank
    lay["ckv_off"] = off
    off += kv_rank
    lay["misc_off"] = off
    off += misc_w
    lay["ncols"] = off
    lay["pe_in_misc"] = n_gate + n_dec + n_icl
    return lay


def _permute_in_cols(w, lay, D):
    W, nd, ni, ng = lay["W"], lay["n_dec"], lay["n_icl"], lay["n_gate"]
    c1 = 3 * W + nd + ni + ng
    c2 = c1 + lay["q_rank"]
    c3 = c2 + lay["kv_rank"] + lay["rope"]
    rkv = w[..., :3 * W]
    xw = w[..., 3 * W:3 * W + nd]
    xa = w[..., 3 * W + nd:3 * W + nd + ni]
    xg = w[..., 3 * W + nd + ni:c1]
    q = w[..., c1:c2]
    ckv = w[..., c2:c2 + lay["kv_rank"]]
    kpe = w[..., c2 + lay["kv_rank"]:c3]
    gates = w[..., c3:]
    pad = jnp.zeros(w.shape[:-1] + (lay["misc_w"] - (ng + nd + ni + lay["rope"]),), w.dtype)
    return jnp.concatenate([rkv, gates, q, ckv, xg, xw, xa, kpe, pad], axis=-1)


def _rope_tables(pos, rope):
    half = rope // 2
    inv = ROPE_THETA ** (-jnp.arange(half, dtype=F32) / half)
    ang = pos.astype(F32)[:, None] * inv[None, :]
    cos, sin = jnp.cos(ang), jnp.sin(ang)
    reps = LANES // rope
    cos_t = jnp.tile(jnp.concatenate([cos, cos], axis=-1), (1, reps))
    sin_t = jnp.tile(jnp.concatenate([-sin, sin], axis=-1), (1, reps))
    return cos_t, sin_t


def kernel(x_prompt, x_sample, cache_kv_latent, cache_k_rope, state_wkv, state_shift, page_table,
           ln_in_g, ln_in_b, w_in, mu_shift, decay_base, w_decay_up, iclr_base, w_iclr_up, w_gate_rw,
           k_k, k_a, r_k, gn_g, gn_b, q_norm_g, w_q_b, kv_norm_g, w_uk, w_uv, w_branch, w_out,
           ln_mix_g, ln_mix_b, w_router, router_bias, w_exp_up, w_exp_down, w_shared_up, w_shared_down,
           ln_ffn_g, ln_ffn_b):
    Bp, Sp, D = x_prompt.shape
    Bs, Ss, _ = x_sample.shape
    L = w_in.shape[0]
    Tp, Ts = Bp * Sp, Bs * Ss
    T = Tp + Ts
    H_rw = r_k.shape[1]
    W = H_rw * HEAD
    n_dec, n_icl, n_gate = w_decay_up.shape[1], w_iclr_up.shape[1], w_gate_rw.shape[1]
    q_rank, H_mla, qk = w_q_b.shape[1], w_q_b.shape[2], w_q_b.shape[3]
    kv_rank, nope = w_uk.shape[1], w_uk.shape[3]
    rope = qk - nope
    E = w_router.shape[2]
    past_len = page_table.shape[1] * cache_kv_latent.shape[2]
    alpha = (2.0 * L) ** 0.25
    scale = float(qk) ** -0.5
    lay = _in_layout(W, n_dec, n_icl, n_gate, q_rank, kv_rank, rope, D)
    assert LANES % H_rw == 0

    pos = jnp.concatenate([jnp.tile(jnp.arange(Sp, dtype=I32), Bp),
                           jnp.tile(past_len + jnp.arange(Ss, dtype=I32), Bs)])
    cos_t, sin_t = _rope_tables(pos, rope)
    cache_kpe_t = jnp.swapaxes(cache_k_rope, 2, 3)

    x_all = jnp.concatenate([x_prompt.reshape(Tp, D), x_sample.reshape(Ts, D)], axis=0)
    h, hb = layer_norm_rows(x_all, ln_in_g, ln_in_b)

    def split_rw(vec):
        rkv = vec[..., :3 * W]
        xw = vec[..., 3 * W:3 * W + n_dec]
        xa = vec[..., 3 * W + n_dec:3 * W + n_dec + n_icl]
        xg = vec[..., 3 * W + n_dec + n_icl:]
        pad = jnp.zeros(vec.shape[:-1] + (lay["misc_w"] - (n_gate + n_dec + n_icl),), vec.dtype)
        return rkv, jnp.concatenate([xg, xw, xa, pad], axis=-1)

    st_p, st_s = [], []
    for l in range(L):
        w_in_b = _permute_in_cols(w_in[l], lay, D).astype(BF16)
        proj = matmul(hb, w_in_b, F32, name="in_proj")

        mu_rkv, mu_misc = split_rw(mu_shift[l][None, :])
        rw_prm = dict(mu_misc=mu_misc, w0=decay_base[l][None, :], w_dec=w_decay_up[l].astype(BF16),
                      a0=iclr_base[l][None, :], w_icl=w_iclr_up[l].astype(BF16), w_g=w_gate_rw[l].astype(BF16))
        mu_t = jnp.stack([_head_tile(mu_rkv[0, i * W:(i + 1) * W], H_rw) for i in range(3)])
        prm_t = jnp.stack([_head_tile(p, H_rw) for p in
                           (k_k[l], k_a[l], r_k[l].reshape(W), gn_g[l], gn_b[l])])

        def rwkv_group(row_off, B, S, shift_prev, wkv_prev):
            N = B * H_rw
            sh_rkv, sh_misc = split_rw(shift_prev)
            dec, a, g = rwkv_prep(proj, row_off, B, S, lay, sh_misc[:, None, :], rw_prm)
            rkv_t = proj[row_off:row_off + B * S, :3 * W].reshape(B, S, 3, H_rw, HEAD)
            rkv_t = rkv_t.transpose(2, 1, 4, 0, 3).reshape(3, S, HEAD, N)
            prev0 = sh_rkv.reshape(B, 3, H_rw, HEAD).transpose(1, 3, 0, 2).reshape(3, HEAD, N)
            s0 = wkv_prev.transpose(2, 3, 0, 1).reshape(HEAD, HEAD, N)
            a_t, dec_t = _to_lanes(a, B, S, H_rw), _to_lanes(dec, B, S, H_rw)
            n_pad = -N % LANES
            if n_pad:
                padl = lambda x: jnp.pad(x, [(0, 0)] * (x.ndim - 1) + [(0, n_pad)])
                rkv_t, a_t, dec_t, prev0, s0 = map(padl, (rkv_t, a_t, dec_t, prev0, s0))
            o_t, s_t = rwkv_scan(rkv_t[0], rkv_t[1], rkv_t[2], a_t, dec_t, prev0, mu_t, prm_t, s0)
            o_a = rwkv_gate(_from_lanes(o_t[..., :N], B, S, H_rw), g)
            wkv_new = s_t[..., :N].reshape(HEAD, HEAD, B, H_rw).transpose(2, 3, 0, 1)
            last = proj[row_off + jnp.arange(B) * S + (S - 1)]
            mo = lay["misc_off"]
            shift_new = jnp.concatenate(
                [last[:, :3 * W], last[:, mo + n_gate:mo + n_gate + n_dec + n_icl], last[:, mo:mo + n_gate]], axis=-1)
            return o_a, wkv_new, shift_new

        zero_shift = jnp.zeros((Bp, mu_shift.shape[1]), F32)
        zero_wkv = jnp.zeros((Bp, H_rw, HEAD, HEAD), F32)
        oa_p, wkv_p, sh_p = rwkv_group(0, Bp, Sp, zero_shift, zero_wkv)
        oa_s, wkv_s, sh_s = rwkv_group(Tp, Bs, Ss, state_shift[l], state_wkv[l])
        o_a = jnp.concatenate([oa_p, oa_s], axis=0)

        qn, ckv_f, ckv_b, kpe_f, kpe_b = mla_prep(proj, lay, q_norm_g[l], kv_norm_g[l], cos_t, sin_t)
        wq = w_q_b[l]
        wq_perm = jnp.concatenate([wq[:, :, :nope].reshape(q_rank, H_mla * nope),
                                   wq[:, :, nope:].reshape(q_rank, H_mla * rope)], axis=1).astype(BF16)
        qa = matmul(qn, wq_perm, F32, name="q_proj")
        w_uk_t = jnp.transpose(w_uk[l], (1, 2, 0)).astype(BF16)
        w_uv_t = jnp.transpose(w_uv[l], (1, 0, 2)).astype(BF16)
        ql_p, qp_p = q_absorb(qa, 0, Tp, H_mla, nope, rope, kv_rank, w_uk_t, cos_t, sin_t, scale, BF16)
        ql_s, qp_s = q_absorb(qa, Tp, Ts, H_mla, nope, rope, kv_rank, w_uk_t, cos_t, sin_t, scale, F32)
        ob_p = attn_prompt(ql_p, qp_p, ckv_b, kpe_b, w_uv_t, Bp, Sp)
        ob_s = attn_sample(ql_s, qp_s, ckv_f, kpe_f, Tp, w_uv_t, cache_kv_latent, cache_kpe_t,
                           page_table, l, Bs, Ss)
        o_b = jnp.concatenate([ob_p, ob_s], axis=0)

        mixin = gated_merge(o_a, o_b, w_branch[l].astype(BF16), proj, lay["gate_off"], D)
        pre = matmul_residual(mixin, w_out[l].astype(BF16), h, alpha)
        x1, x1b, x1p = layer_norm_rows(pre, ln_mix_g[l], ln_mix_b[l], packed=True)

        idx, wts, rank, counts = route(x1, w_router[l], router_bias[l])
        bm = 384 if (T * TOP_K) // E >= 256 else 128
        blocks_per_e = (counts.astype(I32) + bm - 1) // bm
        base = ((jnp.cumsum(blocks_per_e, axis=1) - blocks_per_e) * bm).astype(F32)
        dest = slot_of_assignment(idx, rank, base)
        slot_tok, blk_expert, n_used = _block_tables(counts, dest, T, TOP_K, E, bm)
        x_sorted = x1p[slot_tok]
        y_sorted = expert_ffn(x_sorted, blk_expert, n_used, w_exp_up, w_exp_down, l, bm)
        routed = combine(y_sorted[dest], wts)
        h = shared_ffn_final(x1b, x1, routed, w_shared_up[l].astype(BF16), w_shared_down[l].astype(BF16),
                             ln_ffn_g[l], ln_ffn_b[l], alpha)
        if l + 1 < L:
            hb = h.astype(BF16)

        st_p.append((ckv_f[:Tp].reshape(Bp, Sp, kv_rank), kpe_f[:Tp].reshape(Bp, Sp, rope), wkv_p, sh_p))
        st_s.append((ckv_f[Tp:].reshape(Bs, Ss, kv_rank), kpe_f[Tp:].reshape(Bs, Ss, rope), wkv_s, sh_s))

    y_p = h[:Tp].reshape(Bp, Sp, D)
    y_s = h[Tp:].reshape(Bs, Ss, D)
    stack = lambda sts, k: jnp.stack([s[k] for s in sts])
    return (y_p, y_s, stack(st_p, 0), stack(st_p, 1), stack(st_p, 2), stack(st_p, 3),
            stack(st_s, 0), stack(st_s, 1), stack(st_s, 2), stack(st_s, 3))
```

```python
import functools

import jax
import jax.numpy as jnp
from jax import lax
from jax.experimental import pallas as pl
from jax.experimental.pallas import tpu as pltpu

F32 = jnp.float32
BF16 = jnp.bfloat16
I32 = jnp.int32
U32 = jnp.uint32

TOP_K = 8
N_GROUPS = 8
TOPK_GROUPS = 4
ROUTED_SCALE = 2.5
LN_EPS = 1e-5
RMS_EPS = 1e-6
GN_EPS = 64e-5
ROPE_THETA = 10000.0
KK_NORM_FLOOR = 1e-12

LANES = 128
SUBLANES = 8
VMEM_LIMIT = 56 * 1024 * 1024
HEAD = 64
SCAN_UNROLL = 32
HI16 = 0xFFFF0000


def _cparams(*sem):
    return pltpu.CompilerParams(dimension_semantics=sem, vmem_limit_bytes=VMEM_LIMIT)


def _tile(n, want):
    t = min(n, want)
    while n % t:
        t -= 1
    return t


def _ln_math(x, g, b):
    mu = jnp.mean(x, axis=-1, keepdims=True)
    xc = x - mu
    var = jnp.mean(xc * xc, axis=-1, keepdims=True)
    return xc * lax.rsqrt(var + LN_EPS) * g + b


def _ln_kernel(x_ref, g_ref, b_ref, of_ref, ob_ref):
    y = _ln_math(x_ref[...], g_ref[...], b_ref[...])
    of_ref[...] = y
    ob_ref[...] = y.astype(BF16)


def _ln_pack_kernel(x_ref, g_ref, b_ref, of_ref, ob_ref, op_ref):
    y = _ln_math(x_ref[...], g_ref[...], b_ref[...])
    yb = y.astype(BF16)
    of_ref[...] = y
    ob_ref[...] = yb
    half = y.shape[1] // 2
    bits = pltpu.bitcast(yb.astype(F32), U32)
    op_ref[...] = (bits[:, half:] & jnp.uint32(HI16)) | (bits[:, :half] >> 16)


def layer_norm_rows(x, g, b, packed=False):
    T, D = x.shape
    tm = _tile(T, 256)
    row = pl.BlockSpec((tm, D), lambda i: (i, 0))
    vec = pl.BlockSpec((1, D), lambda i: (0, 0))
    out_specs = [row, row]
    out_shape = [jax.ShapeDtypeStruct((T, D), F32), jax.ShapeDtypeStruct((T, D), BF16)]
    if packed:
        out_specs.append(pl.BlockSpec((tm, D // 2), lambda i: (i, 0)))
        out_shape.append(jax.ShapeDtypeStruct((T, D // 2), U32))
    return pl.pallas_call(
        _ln_pack_kernel if packed else _ln_kernel,
        grid=(T // tm,),
        in_specs=[row, vec, vec],
        out_specs=out_specs,
        out_shape=out_shape,
        compiler_params=_cparams("parallel"),
        name="layer_norm_rows",
    )(x, g.reshape(1, D), b.reshape(1, D))


def _mm_kernel(x_ref, w_ref, o_ref):
    o_ref[...] = jnp.dot(x_ref[...], w_ref[...], preferred_element_type=F32).astype(o_ref.dtype)


def matmul(x, w, out_dtype, tm_want=1024, tn_want=512, name="matmul"):
    M, K = x.shape
    _, N = w.shape
    tm, tn = _tile(M, tm_want), _tile(N, tn_want)
    return pl.pallas_call(
        _mm_kernel,
        grid=(M // tm, N // tn),
        in_specs=[pl.BlockSpec((tm, K), lambda i, j: (i, 0)),
                  pl.BlockSpec((K, tn), lambda i, j: (0, j))],
        out_specs=pl.BlockSpec((tm, tn), lambda i, j: (i, j)),
        out_shape=jax.ShapeDtypeStruct((M, N), out_dtype),
        compiler_params=_cparams("parallel", "arbitrary"),
        name=name,
    )(x, w)


def _rwkv_prep_kernel(misc_ref, misc_prev_ref, sh_ref, mu_ref, w0_ref, wdec_ref, a0_ref, wicl_ref, wg_ref,
                      w_out, a_out, g_out, *, n_gate, n_dec, n_icl):
    p = misc_ref[...]
    prev_row = jnp.where(pl.program_id(1) == 0, sh_ref[0], misc_prev_ref[SUBLANES - 1:SUBLANES, :])
    rolled = pltpu.roll(p, 1, axis=0)
    rid = lax.broadcasted_iota(I32, (p.shape[0], 1), 0)
    p_prev = jnp.where(rid == 0, prev_row, rolled)
    ps = p + (p_prev - p) * mu_ref[...]
    xg = ps[:, :n_gate]
    xw = ps[:, n_gate:n_gate + n_dec]
    xa = ps[:, n_gate + n_dec:n_gate + n_dec + n_icl]

    dec_in = w0_ref[...] + jnp.dot(jnp.tanh(xw).astype(BF16), wdec_ref[...], preferred_element_type=F32)
    z = -dec_in
    softplus = jnp.maximum(z, 0.0) + jnp.log1p(jnp.exp(-jnp.abs(z)))
    w_log = -softplus - 0.5
    w_out[...] = jnp.exp(-jnp.exp(w_log))
    a_out[...] = jax.nn.sigmoid(a0_ref[...] + jnp.dot(xa.astype(BF16), wicl_ref[...], preferred_element_type=F32))
    g_out[...] = jnp.dot(jax.nn.sigmoid(xg).astype(BF16), wg_ref[...], preferred_element_type=F32)


def rwkv_prep(proj, row_off, B, S, lay, shift_misc, prm):
    W = lay["W"]
    Tc = _tile(S, 256)
    nblk = S // Tc
    base = row_off // Tc
    base8 = row_off // SUBLANES
    per8 = Tc // SUBLANES
    mw = lay["misc_w"]
    mi = lay["misc_off"] // mw
    full = lambda shape: pl.BlockSpec(shape, lambda b, i: (0,) * len(shape))
    out_spec = pl.BlockSpec((Tc, W), lambda b, i: (b * nblk + i, 0))
    out_sds = jax.ShapeDtypeStruct((B * S, W), F32)
    kern = functools.partial(_rwkv_prep_kernel, n_gate=lay["n_gate"], n_dec=lay["n_dec"], n_icl=lay["n_icl"])
    return pl.pallas_call(
        kern,
        grid=(B, nblk),
        in_specs=[
            pl.BlockSpec((Tc, mw), lambda b, i: (base + b * nblk + i, mi)),
            pl.BlockSpec((SUBLANES, mw), lambda b, i: (jnp.maximum(base8 + (b * nblk + i) * per8 - 1, 0), mi)),
            pl.BlockSpec((1, 1, mw), lambda b, i: (b, 0, 0)),
            full((1, mw)), full((1, W)), full((lay["n_dec"], W)), full((1, W)), full((lay["n_icl"], W)),
            full((lay["n_gate"], W)),
        ],
        out_specs=[out_spec] * 3,
        out_shape=[out_sds] * 3,
        compiler_params=_cparams("parallel", "arbitrary"),
        name="rwkv_prep",
    )(proj, proj, shift_misc, prm["mu_misc"], prm["w0"], prm["w_dec"], prm["a0"], prm["w_icl"], prm["w_g"])


def _rwkv_scan_kernel(r_ref, k_ref, v_ref, a_ref, w_ref, prev0_ref, mu_ref, prm_ref, s0_ref,
                      o_ref, s_ref, prev_scr, vs_scr, y_scr, *, Tc):
    tc = pl.program_id(1)

    @pl.when(tc == 0)
    def _():
        s_ref[...] = s0_ref[...]
        prev_scr[...] = prev0_ref[...]

    def step(t, carry):
        pr, pk, pv = r_ref[t], k_ref[t], v_ref[t]
        rs = pr + (prev_scr[0] - pr) * mu_ref[0]
        ks = pk + (prev_scr[1] - pk) * mu_ref[1]
        vs = pv + (prev_scr[2] - pv) * mu_ref[2]
        prev_scr[0] = pr
        prev_scr[1] = pk
        prev_scr[2] = pv
        a = a_ref[t]
        w = w_ref[t]
        kk = ks * prm_ref[0]
        ss = jnp.sum(kk * kk, axis=0, keepdims=True)
        kk = kk / jnp.maximum(jnp.sqrt(ss), KK_NORM_FLOOR)
        k2 = ks * (1.0 + (a - 1.0) * prm_ref[1])
        b = kk * a
        kkn = -kk
        vs_scr[...] = vs

        def rows(vg, c):
            for u in range(SCAN_UNROLL):
                vi = vg * SCAN_UNROLL + u
                sv = s_ref[vi]
                sa = jnp.sum(sv * kkn, axis=0, keepdims=True)
                sv = sv * w + sa * b + vs_scr[pl.ds(vi, 1), :] * k2
                s_ref[vi] = sv
                y_scr[pl.ds(vi, 1), :] = jnp.sum(sv * rs, axis=0, keepdims=True)
            return c

        lax.fori_loop(0, HEAD // SCAN_UNROLL, rows, 0)
        y = y_scr[...]
        y_mu = jnp.mean(y, axis=0, keepdims=True)
        yc = y - y_mu
        y_var = jnp.mean(yc * yc, axis=0, keepdims=True)
        yn = yc * lax.rsqrt(y_var + GN_EPS) * prm_ref[3] + prm_ref[4]
        bonus = jnp.sum(rs * k2 * prm_ref[2], axis=0, keepdims=True) * vs
        o_ref[t] = yn + bonus
        return carry

    lax.fori_loop(0, Tc, step, 0)


def rwkv_scan(r, k, v, a, w, prev0, mu, prm, s0):
    S, _, N = r.shape
    Tc = _tile(S, 32)
    seq = pl.BlockSpec((Tc, HEAD, LANES), lambda g, t: (t, 0, g))
    st = pl.BlockSpec((HEAD, HEAD, LANES), lambda g, t: (0, 0, g))
    return pl.pallas_call(
        functools.partial(_rwkv_scan_kernel, Tc=Tc),
        grid=(N // LANES, S // Tc),
        in_specs=[seq] * 5 + [pl.BlockSpec((3, HEAD, LANES), lambda g, t: (0, 0, g)),
                              pl.BlockSpec((3, HEAD, LANES), lambda g, t: (0, 0, 0)),
                              pl.BlockSpec((5, HEAD, LANES), lambda g, t: (0, 0, 0)), st],
        out_specs=[seq, st],
        out_shape=[jax.ShapeDtypeStruct((S, HEAD, N), F32), jax.ShapeDtypeStruct((HEAD, HEAD, N), F32)],
        scratch_shapes=[pltpu.VMEM((3, HEAD, LANES), F32), pltpu.VMEM((HEAD, LANES), F32),
                        pltpu.VMEM((HEAD, LANES), F32)],
        compiler_params=_cparams("parallel", "arbitrary"),
        name="rwkv_scan",
    )(r, k, v, a, w, prev0, mu, prm, s0)


def _gate_kernel(o_ref, g_ref, out_ref):
    out_ref[...] = (o_ref[...] * g_ref[...]).astype(out_ref.dtype)


def rwkv_gate(o_pre, g):
    T, W = o_pre.shape
    tm = _tile(T, 512)
    row = pl.BlockSpec((tm, W), lambda i: (i, 0))
    return pl.pallas_call(
        _gate_kernel, grid=(T // tm,), in_specs=[row, row], out_specs=row,
        out_shape=jax.ShapeDtypeStruct((T, W), BF16),
        compiler_params=_cparams("parallel"), name="rwkv_gate",
    )(o_pre, g)


def _to_lanes(x, B, S, H):
    return x.reshape(B, S, H, HEAD).transpose(1, 3, 0, 2).reshape(S, HEAD, B * H)


def _from_lanes(x, B, S, H):
    return x.reshape(S, HEAD, B, H).transpose(2, 0, 3, 1).reshape(B * S, H * HEAD)


def _head_tile(p, H):
    return jnp.tile(p.reshape(H, HEAD).T, (1, LANES // H))


def _swap_halves(x, width):
    n = x.shape[1]
    half = width // 2
    lane = lax.broadcasted_iota(I32, x.shape, 1)
    fwd = pltpu.roll(x, half, axis=1)
    bwd = pltpu.roll(x, n - half, axis=1)
    return jnp.where(lane % width < half, bwd, fwd)


def _mla_prep_kernel(pq_ref, ckv_ref, misc_ref, qg_ref, kvg_ref, cos_ref, sin_ref,
                     qn_ref, ckv_f_ref, ckv_b_ref, kpe_f_ref, kpe_b_ref, *, pe_off, rope):
    def rms(x, g):
        return x * lax.rsqrt(jnp.mean(x * x, axis=-1, keepdims=True) + RMS_EPS) * g

    qn_ref[...] = rms(pq_ref[...], qg_ref[...]).astype(BF16)
    ckv = rms(ckv_ref[...], kvg_ref[...])
    ckv_f_ref[...] = ckv
    ckv_b_ref[...] = ckv.astype(BF16)
    blk = misc_ref[:, pe_off - pe_off % LANES: pe_off - pe_off % LANES + LANES]
    roped = blk * cos_ref[...] + _swap_halves(blk, rope) * sin_ref[...]
    kpe = roped[:, pe_off % LANES: pe_off % LANES + rope]
    kpe_f_ref[...] = kpe
    kpe_b_ref[...] = kpe.astype(BF16)


def mla_prep(proj, lay, q_norm_g, kv_norm_g, cos_t, sin_t):
    T = proj.shape[0]
    QR, KR, rope, mw = lay["q_rank"], lay["kv_rank"], lay["rope"], lay["misc_w"]
    tm = _tile(T, 256)
    row = lambda wd, col: pl.BlockSpec((tm, wd), lambda i: (i, col))
    vec = lambda wd: pl.BlockSpec((1, wd), lambda i: (0, 0))
    kern = functools.partial(_mla_prep_kernel, pe_off=lay["pe_in_misc"], rope=rope)
    return pl.pallas_call(
        kern,
        grid=(T // tm,),
        in_specs=[row(QR, lay["q_off"] // QR), row(KR, lay["ckv_off"] // KR), row(mw, lay["misc_off"] // mw),
                  vec(QR), vec(KR), row(LANES, 0), row(LANES, 0)],
        out_specs=[row(QR, 0), row(KR, 0), row(KR, 0), row(rope, 0), row(rope, 0)],
        out_shape=[jax.ShapeDtypeStruct((T, QR), BF16), jax.ShapeDtypeStruct((T, KR), F32),
                   jax.ShapeDtypeStruct((T, KR), BF16), jax.ShapeDtypeStruct((T, rope), F32),
                   jax.ShapeDtypeStruct((T, rope), BF16)],
        compiler_params=_cparams("parallel"),
        name="mla_prep",
    )(proj, proj, proj, q_norm_g.reshape(1, QR), kv_norm_g.reshape(1, KR), cos_t, sin_t)


def _q_absorb_kernel(nope_ref, pe_ref, wuk_ref, cos_ref, sin_ref, ql_ref, qp_ref, *, nope, rope, scale):
    hp = LANES // rope
    for h in range(hp):
        qn = nope_ref[:, h * nope:(h + 1) * nope].astype(BF16)
        ql = jnp.dot(qn, wuk_ref[h], preferred_element_type=F32) * scale
        ql_ref[h] = ql.astype(ql_ref.dtype)
    pe = pe_ref[...]
    roped = (pe * cos_ref[...] + _swap_halves(pe, rope) * sin_ref[...]) * scale
    for h in range(hp):
        qp_ref[h] = roped[:, h * rope:(h + 1) * rope].astype(qp_ref.dtype)


def q_absorb(qa, row_off, Tg, H, nope, rope, kv_rank, w_uk_t, cos_t, sin_t, scale, out_dtype):
    hp = LANES // rope
    tm = _tile(Tg, 256)
    base = row_off // tm
    pe_col0 = (H * nope) // LANES
    kern = functools.partial(_q_absorb_kernel, nope=nope, rope=rope, scale=scale)
    return pl.pallas_call(
        kern,
        grid=(Tg // tm, H // hp),
        in_specs=[pl.BlockSpec((tm, hp * nope), lambda i, j: (base + i, j)),
                  pl.BlockSpec((tm, LANES), lambda i, j: (base + i, pe_col0 + j)),
                  pl.BlockSpec((hp, nope, kv_rank), lambda i, j: (j, 0, 0)),
                  pl.BlockSpec((tm, LANES), lambda i, j: (base + i, 0)),
                  pl.BlockSpec((tm, LANES), lambda i, j: (base + i, 0))],
        out_specs=[pl.BlockSpec((hp, tm, kv_rank), lambda i, j: (j, i, 0)),
                   pl.BlockSpec((hp, tm, rope), lambda i, j: (j, i, 0))],
        out_shape=[jax.ShapeDtypeStruct((H, Tg, kv_rank), out_dtype),
                   jax.ShapeDtypeStruct((H, Tg, rope), out_dtype)],
        compiler_params=_cparams("parallel", "arbitrary"),
        name="q_absorb",
    )(qa, qa, w_uk_t, cos_t, sin_t)


_NT = (((1,), (1,)), ((), ()))


def _attn_prompt_kernel(ql_ref, qp_ref, ckv_ref, kpe_ref, wuv_ref, o_ref, m_ref, l_ref, acc_ref, *, H, TQ, CK, vdim):
    i = pl.program_id(1)
    R = H * TQ
    q = ql_ref[...].reshape(R, ql_ref.shape[-1])
    qp = qp_ref[...].reshape(R, qp_ref.shape[-1])
    m_ref[...] = jnp.full(m_ref.shape, -jnp.inf, F32)
    l_ref[...] = jnp.zeros(l_ref.shape, F32)
    acc_ref[...] = jnp.zeros(acc_ref.shape, F32)
    q_pos = i * TQ + lax.broadcasted_iota(I32, (R, 1), 0) % TQ
    n_chunks = ((i + 1) * TQ + CK - 1) // CK

    def chunk(j, carry):
        ks = pl.ds(pl.multiple_of(j * CK, CK), CK)
        kc = ckv_ref[ks, :]
        s = lax.dot_general(q, kc, _NT, preferred_element_type=F32)
        s = s + lax.dot_general(qp, kpe_ref[ks, :], _NT, preferred_element_type=F32)
        key_pos = j * CK + lax.broadcasted_iota(I32, (1, CK), 1)
        s = jnp.where(key_pos <= q_pos, s, -jnp.inf)
        m_old = m_ref[...]
        m_new = jnp.maximum(m_old, jnp.max(s, axis=-1, keepdims=True))
        corr = jnp.exp(m_old - m_new)
        p = jnp.exp(s - m_new)
        l_ref[...] = l_ref[...] * corr + jnp.sum(p, axis=-1, keepdims=True)
        acc_ref[...] = acc_ref[...] * corr + jnp.dot(p.astype(BF16), kc, preferred_element_type=F32)
        m_ref[...] = m_new
        return carry

    lax.fori_loop(0, n_chunks, chunk, 0)
    o = (acc_ref[...] / l_ref[...]).astype(BF16)
    for h in range(H):
        o_ref[:, h * vdim:(h + 1) * vdim] = jnp.dot(
            o[h * TQ:(h + 1) * TQ, :], wuv_ref[h], preferred_element_type=F32).astype(o_ref.dtype)


def attn_prompt(q_lat, q_pe, ckv_b, kpe_b, w_uv_t, B, S):
    H, _, KR = q_lat.shape
    rope = q_pe.shape[-1]
    vdim = w_uv_t.shape[-1]
    TQ = _tile(S, 64)
    CK = _tile(S, 512)
    nq = S // TQ
    kern = functools.partial(_attn_prompt_kernel, H=H, TQ=TQ, CK=CK, vdim=vdim)
    return pl.pallas_call(
        kern,
        grid=(B, nq),
        in_specs=[pl.BlockSpec((H, TQ, KR), lambda b, i: (0, b * nq + i, 0)),
                  pl.BlockSpec((H, TQ, rope), lambda b, i: (0, b * nq + i, 0)),
                  pl.BlockSpec((S, KR), lambda b, i: (b, 0)),
                  pl.BlockSpec((S, rope), lambda b, i: (b, 0)),
                  pl.BlockSpec((H, KR, vdim), lambda b, i: (0, 0, 0))],
        out_specs=pl.BlockSpec((TQ, H * vdim), lambda b, i: (b * nq + i, 0)),
        out_shape=jax.ShapeDtypeStruct((B * S, H * vdim), BF16),
        scratch_shapes=[pltpu.VMEM((H * TQ, 1), F32), pltpu.VMEM((H * TQ, 1), F32),
                        pltpu.VMEM((H * TQ, KR), F32)],
        compiler_params=_cparams("parallel", "arbitrary"),
        name="attn_prompt",
    )(q_lat, q_pe, ckv_b, kpe_b, w_uv_t)


def _attn_sample_kernel(pt_ref, ql_ref, qp_ref, ckv_ref, kpe_ref, wuv_ref, *rest, H, TS, G, vdim):
    kv_refs = rest[:G]
    kp_refs = rest[G:2 * G]
    o_ref, m_ref, l_ref, acc_ref = rest[2 * G:]
    g = pl.program_id(1)
    R = H * TS
    q32 = ql_ref[...].reshape(R, ql_ref.shape[-1])
    qp32 = qp_ref[...].reshape(R, qp_ref.shape[-1])
    q = q32.astype(BF16)
    qp = qp32.astype(BF16)

    @pl.when(g == 0)
    def _():
        ck = ckv_ref[...]
        s = lax.dot_general(q32, ck, _NT, preferred_element_type=F32)
        s = s + lax.dot_general(qp32, kpe_ref[...], _NT, preferred_element_type=F32)
        t_row = lax.broadcasted_iota(I32, (R, 1), 0) % TS
        t_key = lax.broadcasted_iota(I32, (1, TS), 1)
        s = jnp.where(t_key <= t_row, s, -jnp.inf)
        m0 = jnp.max(s, axis=-1, keepdims=True)
        p = jnp.exp(s - m0)
        m_ref[...] = m0
        l_ref[...] = jnp.sum(p, axis=-1, keepdims=True)
        acc_ref[...] = jnp.dot(p, ck, preferred_element_type=F32)

    pages = [kv_refs[j][0, 0].astype(BF16) for j in range(G)]
    s_parts = []
    for j in range(G):
        sj = lax.dot_general(q, pages[j], _NT, preferred_element_type=F32)
        sj = sj + jnp.dot(qp, kp_refs[j][0, 0].astype(BF16), preferred_element_type=F32)
        s_parts.append(sj)
    s = jnp.concatenate(s_parts, axis=1) if G > 1 else s_parts[0]
    m_old = m_ref[...]
    m_new = jnp.maximum(m_old, jnp.max(s, axis=-1, keepdims=True))
    corr = jnp.exp(m_old - m_new)
    p = jnp.exp(s - m_new)
    l_ref[...] = l_ref[...] * corr + jnp.sum(p, axis=-1, keepdims=True)
    pb = p.astype(BF16)
    P = pages[0].shape[0]
    pv = jnp.dot(pb[:, :P], pages[0], preferred_element_type=F32)
    for j in range(1, G):
        pv = pv + jnp.dot(pb[:, j * P:(j + 1) * P], pages[j], preferred_element_type=F32)
    acc_ref[...] = acc_ref[...] * corr + pv
    m_ref[...] = m_new

    @pl.when(g == pl.num_programs(1) - 1)
    def _():
        o = (acc_ref[...] / l_ref[...]).astype(BF16)
        for h in range(H):
            o_ref[:, h * vdim:(h + 1) * vdim] = jnp.dot(
                o[h * TS:(h + 1) * TS, :], wuv_ref[h], preferred_element_type=F32).astype(o_ref.dtype)


def attn_sample(q_lat, q_pe, ckv_f, kpe_f, row_off, w_uv_t, cache_kv, cache_kpe_t, page_table, layer, DB, TS):
    H, _, KR = q_lat.shape
    rope = q_pe.shape[-1]
    vdim = w_uv_t.shape[-1]
    n_pages = page_table.shape[1]
    P = cache_kv.shape[2]
    G = _tile(n_pages, 16)
    base = row_off // TS

    def page_spec(shape, j):
        return pl.BlockSpec((1, 1) + shape, lambda b, g, pt: (layer, pt[b, g * G + j], 0, 0))

    kern = functools.partial(_attn_sample_kernel, H=H, TS=TS, G=G, vdim=vdim)
    grid_spec = pltpu.PrefetchScalarGridSpec(
        num_scalar_prefetch=1,
        grid=(DB, n_pages // G),
        in_specs=[pl.BlockSpec((H, TS, KR), lambda b, g, pt: (0, b, 0)),
                  pl.BlockSpec((H, TS, rope), lambda b, g, pt: (0, b, 0)),
                  pl.BlockSpec((TS, KR), lambda b, g, pt: (base + b, 0)),
                  pl.BlockSpec((TS, rope), lambda b, g, pt: (base + b, 0)),
                  pl.BlockSpec((H, KR, vdim), lambda b, g, pt: (0, 0, 0))]
                 + [page_spec((P, KR), j) for j in range(G)] + [page_spec((rope, P), j) for j in range(G)],
        out_specs=pl.BlockSpec((TS, H * vdim), lambda b, g, pt: (b, 0)),
        scratch_shapes=[pltpu.VMEM((H * TS, 1), F32), pltpu.VMEM((H * TS, 1), F32),
                        pltpu.VMEM((H * TS, KR), F32)],
    )
    return pl.pallas_call(
        kern,
        grid_spec=grid_spec,
        out_shape=jax.ShapeDtypeStruct((DB * TS, H * vdim), BF16),
        compiler_params=_cparams("parallel", "arbitrary"),
        name="attn_sample",
    )(page_table, q_lat, q_pe, ckv_f, kpe_f, w_uv_t, *([cache_kv] * G), *([cache_kpe_t] * G))


def _merge_kernel(oa_ref, ob_ref, wa_ref, wb_ref, ga_ref, gb_ref, o_ref):
    a = jnp.dot(oa_ref[...], wa_ref[...], preferred_element_type=F32)
    b = jnp.dot(ob_ref[...], wb_ref[...], preferred_element_type=F32)
    o_ref[...] = (jax.nn.sigmoid(ga_ref[...]) * a + jax.nn.sigmoid(gb_ref[...]) * b).astype(o_ref.dtype)


def gated_merge(o_a, o_b, w_branch_b, proj, gate_off, D):
    T, Wb = o_a.shape
    tm, tn = _tile(T, 1024), _tile(D, 512)
    g0 = gate_off // tn
    g1 = (gate_off + D) // tn
    return pl.pallas_call(
        _merge_kernel,
        grid=(T // tm, D // tn),
        in_specs=[pl.BlockSpec((tm, Wb), lambda i, j: (i, 0)),
                  pl.BlockSpec((tm, Wb), lambda i, j: (i, 0)),
                  pl.BlockSpec((None, Wb, tn), lambda i, j: (0, 0, j)),
                  pl.BlockSpec((None, Wb, tn), lambda i, j: (1, 0, j)),
                  pl.BlockSpec((tm, tn), lambda i, j: (i, g0 + j)),
                  pl.BlockSpec((tm, tn), lambda i, j: (i, g1 + j))],
        out_specs=pl.BlockSpec((tm, tn), lambda i, j: (i, j)),
        out_shape=jax.ShapeDtypeStruct((T, D), BF16),
        compiler_params=_cparams("parallel", "arbitrary"),
        name="gated_merge",
    )(o_a, o_b, w_branch_b, w_branch_b, proj, proj)


def _mm_resid_kernel(x_ref, w_ref, h_ref, o_ref, *, alpha):
    o_ref[...] = alpha * h_ref[...] + jnp.dot(x_ref[...], w_ref[...], preferred_element_type=F32)


def matmul_residual(x, w, h, alpha):
    M, K = x.shape
    N = w.shape[1]
    tm, tn = _tile(M, 1024), _tile(N, 512)
    return pl.pallas_call(
        functools.partial(_mm_resid_kernel, alpha=alpha),
        grid=(M // tm, N // tn),
        in_specs=[pl.BlockSpec((tm, K), lambda i, j: (i, 0)),
                  pl.BlockSpec((K, tn), lambda i, j: (0, j)),
                  pl.BlockSpec((tm, tn), lambda i, j: (i, j))],
        out_specs=pl.BlockSpec((tm, tn), lambda i, j: (i, j)),
        out_shape=jax.ShapeDtypeStruct((M, N), F32),
        compiler_params=_cparams("parallel", "arbitrary"),
        name="matmul_residual",
    )(x, w, h)


def _first_index(hit, lane, size):
    return jnp.min(jnp.where(hit, lane, size), axis=1, keepdims=True)


def _route_kernel(x_ref, wh_ref, wl_ref, bias_ref, tri_ref, idx_ref, wts_ref, rank_ref, cnt_ref, run_ref,
                  *, n_groups, topk_groups, top_k):
    @pl.when(pl.program_id(0) == 0)
    def _():
        run_ref[...] = jnp.zeros(run_ref.shape, F32)

    x = x_ref[...]
    xh = x.astype(BF16)
    xl = (x - xh.astype(F32)).astype(BF16)
    wh = wh_ref[...]
    logits = jnp.dot(xh, wh, preferred_element_type=F32)
    logits = logits + jnp.dot(xl, wh, preferred_element_type=F32)
    logits = logits + jnp.dot(xh, wl_ref[...], preferred_element_type=F32)
    s = jax.nn.sigmoid(logits)
    sc = s + bias_ref[...]
    tm, E = s.shape
    gsz = E // n_groups
    neg = -jnp.inf
    lane = lax.broadcasted_iota(I32, (tm, E), 1)
    gid = lane // gsz
    lane_s = lax.broadcasted_iota(I32, (tm, LANES), 1)

    grp = jnp.full((tm, LANES), neg, F32)
    for g in range(n_groups):
        m = jnp.where(gid == g, sc, neg)
        t1 = jnp.max(m, axis=1, keepdims=True)
        i1 = _first_index(m == t1, lane, E)
        t2 = jnp.max(jnp.where(lane == i1, neg, m), axis=1, keepdims=True)
        grp = jnp.where(lane_s == g, t1 + t2, grp)

    allowed = jnp.zeros((tm, E), I32)
    for _ in range(topk_groups):
        mx = jnp.max(grp, axis=1, keepdims=True)
        gi = _first_index(grp == mx, lane_s, LANES)
        grp = jnp.where(lane_s == gi, neg, grp)
        allowed = jnp.where(gid == gi, 1, allowed)
    masked = jnp.where(allowed > 0, sc, neg)

    onehot = jnp.zeros((tm, E), F32)
    idx_t = jnp.zeros((tm, LANES), I32)
    w_t = jnp.zeros((tm, LANES), F32)
    picks = []
    for k in range(top_k):
        mx = jnp.max(masked, axis=1, keepdims=True)
        ei = _first_index(masked == mx, lane, E)
        hit = lane == ei
        masked = jnp.where(hit, neg, masked)
        onehot = jnp.where(hit, 1.0, onehot)
        sv = jnp.sum(jnp.where(hit, s, 0.0), axis=1, keepdims=True)
        idx_t = jnp.where(lane_s == k, ei, idx_t)
        w_t = jnp.where(lane_s == k, sv, w_t)
        picks.append(hit)
    wts = w_t / jnp.sum(w_t, axis=1, keepdims=True) * ROUTED_SCALE

    before = jnp.dot(tri_ref[...], onehot.astype(BF16), preferred_element_type=F32) + run_ref[...]
    rank_t = jnp.zeros((tm, LANES), F32)
    for k in range(top_k):
        rk = jnp.sum(jnp.where(picks[k], before, 0.0), axis=1, keepdims=True)
        rank_t = jnp.where(lane_s == k, rk, rank_t)
    run_ref[...] += jnp.sum(onehot, axis=0, keepdims=True)

    idx_ref[...] = idx_t[:, :top_k]
    wts_ref[...] = wts[:, :top_k]
    rank_ref[...] = rank_t[:, :top_k].astype(I32)
    cnt_ref[...] = run_ref[...]


def route(x, w_router, router_bias):
    T, D = x.shape
    E = w_router.shape[1]
    wh = w_router.astype(BF16)
    wl = (w_router - wh.astype(F32)).astype(BF16)
    tm = _tile(T, 256)
    tri = (jnp.arange(tm)[:, None] > jnp.arange(tm)[None, :]).astype(BF16)
    kern = functools.partial(_route_kernel, n_groups=N_GROUPS, topk_groups=TOPK_GROUPS, top_k=TOP_K)
    small = pl.BlockSpec((tm, TOP_K), lambda i: (i, 0))
    return pl.pallas_call(
        kern,
        grid=(T // tm,),
        in_specs=[pl.BlockSpec((tm, D), lambda i: (i, 0)),
                  pl.BlockSpec((D, E), lambda i: (0, 0)),
                  pl.BlockSpec((D, E), lambda i: (0, 0)),
                  pl.BlockSpec((1, E), lambda i: (0, 0)),
                  pl.BlockSpec((tm, tm), lambda i: (0, 0))],
        out_specs=[small, small, small, pl.BlockSpec((1, E), lambda i: (0, 0))],
        out_shape=[jax.ShapeDtypeStruct((T, TOP_K), I32), jax.ShapeDtypeStruct((T, TOP_K), F32),
                   jax.ShapeDtypeStruct((T, TOP_K), I32), jax.ShapeDtypeStruct((1, E), F32)],
        scratch_shapes=[pltpu.VMEM((1, E), F32)],
        compiler_params=_cparams("arbitrary"),
        name="route",
    )(x, wh, wl, router_bias.reshape(1, E).astype(F32), tri)


def _slot_kernel(idx_ref, rank_ref, base_ref, dest_ref):
    idx = idx_ref[...]
    tm, K = idx.shape
    E = base_ref.shape[1]
    lane = lax.broadcasted_iota(I32, (tm, E), 1)
    lane_s = lax.broadcasted_iota(I32, (tm, LANES), 1)
    base = base_ref[...]
    out = jnp.zeros((tm, LANES), F32)
    for k in range(K):
        bk = jnp.sum(jnp.where(lane == idx[:, k:k + 1], base, 0.0), axis=1, keepdims=True)
        out = jnp.where(lane_s == k, bk, out)
    dest_ref[...] = out[:, :K].astype(I32) + rank_ref[...]


def slot_of_assignment(idx, rank, base):
    T, K = idx.shape
    E = base.shape[1]
    tm = _tile(T, 512)
    small = pl.BlockSpec((tm, K), lambda i: (i, 0))
    return pl.pallas_call(
        _slot_kernel, grid=(T // tm,),
        in_specs=[small, small, pl.BlockSpec((1, E), lambda i: (0, 0))],
        out_specs=small, out_shape=jax.ShapeDtypeStruct((T, K), I32),
        compiler_params=_cparams("parallel"), name="slot_of_assignment",
    )(idx, rank, base)


def _expert_kernel(be_ref, nu_ref, x_ref, wg_ref, wu_ref, wd_ref, o_ref):
    blk = pl.program_id(0)
    j = pl.program_id(1)

    @pl.when(blk < nu_ref[0])
    def _():
        xu = x_ref[...]
        half = xu.shape[1]
        x_lo = pltpu.bitcast(xu << 16, F32).astype(BF16)
        x_hi = pltpu.bitcast(xu & jnp.uint32(HI16), F32).astype(BF16)

        def up(w_ref):
            w = w_ref[0, 0].astype(BF16)
            return (jnp.dot(x_lo, w[:half], preferred_element_type=F32)
                    + jnp.dot(x_hi, w[half:], preferred_element_type=F32))

        hg = up(wg_ref)
        hu = up(wu_ref)
        act = (hg * jax.nn.sigmoid(hg) * hu).astype(BF16)
        part = jnp.dot(act, wd_ref[0, 0].astype(BF16), preferred_element_type=F32)

        @pl.when(j == 0)
        def _():
            o_ref[...] = part

        @pl.when(j > 0)
        def _():
            o_ref[...] += part

    @pl.when(jnp.logical_and(blk >= nu_ref[0], j == 0))
    def _():
        o_ref[...] = jnp.zeros(o_ref.shape, F32)


def expert_ffn(x_sorted, blk_expert, n_used, w_exp_up, w_exp_down, layer, bm):
    n_slots, Dh = x_sorted.shape
    D = 2 * Dh
    DE = w_exp_down.shape[2]
    J = 2 if DE % (2 * LANES) == 0 else 1
    dh = DE // J
    n_blocks = n_slots // bm

    def live(blk, nu):
        return jnp.minimum(blk, nu[0] - 1)

    def jj(blk, j, nu):
        return jnp.where(blk < nu[0], j, J - 1)

    grid_spec = pltpu.PrefetchScalarGridSpec(
        num_scalar_prefetch=2,
        grid=(n_blocks, J),
        in_specs=[pl.BlockSpec((bm, Dh), lambda b, j, be, nu: (live(b, nu), 0)),
                  pl.BlockSpec((1, 1, D, dh), lambda b, j, be, nu: (layer, be[live(b, nu)], 0, jj(b, j, nu))),
                  pl.BlockSpec((1, 1, D, dh), lambda b, j, be, nu: (layer, be[live(b, nu)], 0, J + jj(b, j, nu))),
                  pl.BlockSpec((1, 1, dh, D), lambda b, j, be, nu: (layer, be[live(b, nu)], jj(b, j, nu), 0))],
        out_specs=pl.BlockSpec((bm, D), lambda b, j, be, nu: (b, 0)),
    )
    return pl.pallas_call(
        _expert_kernel,
        grid_spec=grid_spec,
        out_shape=jax.ShapeDtypeStruct((n_slots, D), F32),
        compiler_params=_cparams("arbitrary", "arbitrary"),
        name="expert_ffn",
    )(blk_expert, n_used, x_sorted, w_exp_up, w_exp_up, w_exp_down)


def _combine_kernel(w_ref, *refs):
    y_refs, o_ref = refs[:-1], refs[-1]
    w = w_ref[...]
    acc = y_refs[0][...] * w[:, 0:1]
    for k in range(1, len(y_refs)):
        acc = acc + y_refs[k][...] * w[:, k:k + 1]
    o_ref[...] = acc


def combine(y_tok, wts):
    K, T, D = y_tok.shape
    tm = _tile(T, 128)
    return pl.pallas_call(
        _combine_kernel, grid=(T // tm,),
        in_specs=[pl.BlockSpec((tm, K), lambda i: (i, 0))]
                 + [pl.BlockSpec((None, tm, D), lambda i, k=k: (k, i, 0)) for k in range(K)],
        out_specs=pl.BlockSpec((tm, D), lambda i: (i, 0)),
        out_shape=jax.ShapeDtypeStruct((T, D), F32),
        compiler_params=_cparams("parallel"), name="combine",
    )(wts, *([y_tok] * K))


def _shared_final_kernel(xb_ref, x_ref, routed_ref, wu_ref, wd_ref, g_ref, b_ref, o_ref, *, alpha, DE):
    hs = jnp.dot(xb_ref[...], wu_ref[...], preferred_element_type=F32)
    hg, hu = hs[:, :DE], hs[:, DE:]
    act = (hg * jax.nn.sigmoid(hg) * hu).astype(BF16)
    shared = jnp.dot(act, wd_ref[...], preferred_element_type=F32)
    pre = alpha * x_ref[...] + (routed_ref[...] + shared)
    o_ref[...] = _ln_math(pre, g_ref[...], b_ref[...])


def shared_ffn_final(xb, x, routed, w_up_b, w_down_b, g, b, alpha):
    T, D = x.shape
    DE = w_down_b.shape[0]
    tm = _tile(T, 128)
    row = lambda: pl.BlockSpec((tm, D), lambda i: (i, 0))
    vec = pl.BlockSpec((1, D), lambda i: (0, 0))
    return pl.pallas_call(
        functools.partial(_shared_final_kernel, alpha=alpha, DE=DE),
        grid=(T // tm,),
        in_specs=[row(), row(), row(),
                  pl.BlockSpec((D, 2 * DE), lambda i: (0, 0)),
                  pl.BlockSpec((DE, D), lambda i: (0, 0)), vec, vec],
        out_specs=row(),
        out_shape=jax.ShapeDtypeStruct((T, D), F32),
        compiler_params=_cparams("parallel"),
        name="shared_ffn_final",
    )(xb, x, routed, w_up_b, w_down_b, g.reshape(1, D), b.reshape(1, D))


def _block_tables(counts, dest, T, K, E, bm):
    M = T * K
    n_blocks = -(-M // bm) + E
    n_slots = n_blocks * bm
    cnt = counts.reshape(E).astype(I32)
    blocks_per_e = (cnt + bm - 1) // bm
    blk_end = jnp.cumsum(blocks_per_e)
    blk_expert = jnp.minimum(jnp.searchsorted(blk_end, jnp.arange(n_blocks), side="right"), E - 1).astype(I32)
    n_used = blk_end[-1:].astype(I32)
    slot_tok = (jnp.arange(n_slots, dtype=I32) % T).at[dest.reshape(M)].set(jnp.arange(M, dtype=I32) // K)
    return slot_tok, blk_expert, n_used


def _in_layout(W, n_dec, n_icl, n_gate, q_rank, kv_rank, rope, D):
    misc_w = -(-(n_gate + n_dec + n_icl + rope) // LANES) * LANES
    lay = dict(W=W, n_dec=n_dec, n_icl=n_icl, n_gate=n_gate, q_rank=q_rank, kv_rank=kv_rank, rope=rope,
               misc_w=misc_w)
    off = 3 * W
    lay["gate_off"] = off
    off += 2 * D
    lay["q_off"] = off
    off += q_rank
    lay["ckv_off"] = off
    off += kv_rank
    lay["misc_off"] = off
    off += misc_w
    lay["ncols"] = off
    lay["pe_in_misc"] = n_gate + n_dec + n_icl
    return lay


def _permute_in_cols(w, lay, D):
    W, nd, ni, ng = lay["W"], lay["n_dec"], lay["n_icl"], lay["n_gate"]
    c1 = 3 * W + nd + ni + ng
    c2 = c1 + lay["q_rank"]
    c3 = c2 + lay["kv_rank"] + lay["rope"]
    rkv = w[..., :3 * W]
    xw = w[..., 3 * W:3 * W + nd]
    xa = w[..., 3 * W + nd:3 * W + nd + ni]
    xg = w[..., 3 * W + nd + ni:c1]
    q = w[..., c1:c2]
    ckv = w[..., c2:c2 + lay["kv_rank"]]
    kpe = w[..., c2 + lay["kv_rank"]:c3]
    gates = w[..., c3:]
    pad = jnp.zeros(w.shape[:-1] + (lay["misc_w"] - (ng + nd + ni + lay["rope"]),), w.dtype)
    return jnp.concatenate([rkv, gates, q, ckv, xg, xw, xa, kpe, pad], axis=-1)


def _rope_tables(pos, rope):
    half = rope // 2
    inv = ROPE_THETA ** (-jnp.arange(half, dtype=F32) / half)
    ang = pos.astype(F32)[:, None] * inv[None, :]
    cos, sin = jnp.cos(ang), jnp.sin(ang)
    reps = LANES // rope
    cos_t = jnp.tile(jnp.concatenate([cos, cos], axis=-1), (1, reps))
    sin_t = jnp.tile(jnp.concatenate([-sin, sin], axis=-1), (1, reps))
    return cos_t, sin_t


def kernel(x_prompt, x_sample, cache_kv_latent, cache_k_rope, state_wkv, state_shift, page_table,
           ln_in_g, ln_in_b, w_in, mu_shift, decay_base, w_decay_up, iclr_base, w_iclr_up, w_gate_rw,
           k_k, k_a, r_k, gn_g, gn_b, q_norm_g, w_q_b, kv_norm_g, w_uk, w_uv, w_branch, w_out,
           ln_mix_g, ln_mix_b, w_router, router_bias, w_exp_up, w_exp_down, w_shared_up, w_shared_down,
           ln_ffn_g, ln_ffn_b):
    Bp, Sp, D = x_prompt.shape
    Bs, Ss, _ = x_sample.shape
    L = w_in.shape[0]
    Tp, Ts = Bp * Sp, Bs * Ss
    T = Tp + Ts
    H_rw = r_k.shape[1]
    W = H_rw * HEAD
    n_dec, n_icl, n_gate = w_decay_up.shape[1], w_iclr_up.shape[1], w_gate_rw.shape[1]
    q_rank, H_mla, qk = w_q_b.shape[1], w_q_b.shape[2], w_q_b.shape[3]
    kv_rank, nope = w_uk.shape[1], w_uk.shape[3]
    rope = qk - nope
    E = w_router.shape[2]
    past_len = page_table.shape[1] * cache_kv_latent.shape[2]
    alpha = (2.0 * L) ** 0.25
    scale = float(qk) ** -0.5
    lay = _in_layout(W, n_dec, n_icl, n_gate, q_rank, kv_rank, rope, D)
    assert LANES % H_rw == 0

    pos = jnp.concatenate([jnp.tile(jnp.arange(Sp, dtype=I32), Bp),
                           jnp.tile(past_len + jnp.arange(Ss, dtype=I32), Bs)])
    cos_t, sin_t = _rope_tables(pos, rope)
    cache_kpe_t = jnp.swapaxes(cache_k_rope, 2, 3)

    x_all = jnp.concatenate([x_prompt.reshape(Tp, D), x_sample.reshape(Ts, D)], axis=0)
    h, hb = layer_norm_rows(x_all, ln_in_g, ln_in_b)

    def split_rw(vec):
        rkv = vec[..., :3 * W]
        xw = vec[..., 3 * W:3 * W + n_dec]
        xa = vec[..., 3 * W + n_dec:3 * W + n_dec + n_icl]
        xg = vec[..., 3 * W + n_dec + n_icl:]
        pad = jnp.zeros(vec.shape[:-1] + (lay["misc_w"] - (n_gate + n_dec + n_icl),), vec.dtype)
        return rkv, jnp.concatenate([xg, xw, xa, pad], axis=-1)

    st_p, st_s = [], []
    for l in range(L):
        w_in_b = _permute_in_cols(w_in[l], lay, D).astype(BF16)
        proj = matmul(hb, w_in_b, F32, name="in_proj")

        mu_rkv, mu_misc = split_rw(mu_shift[l][None, :])
        rw_prm = dict(mu_misc=mu_misc, w0=decay_base[l][None, :], w_dec=w_decay_up[l].astype(BF16),
                      a0=iclr_base[l][None, :], w_icl=w_iclr_up[l].astype(BF16), w_g=w_gate_rw[l].astype(BF16))
        mu_t = jnp.stack([_head_tile(mu_rkv[0, i * W:(i + 1) * W], H_rw) for i in range(3)])
        prm_t = jnp.stack([_head_tile(p, H_rw) for p in
                           (k_k[l], k_a[l], r_k[l].reshape(W), gn_g[l], gn_b[l])])

        def rwkv_group(row_off, B, S, shift_prev, wkv_prev):
            N = B * H_rw
            sh_rkv, sh_misc = split_rw(shift_prev)
            dec, a, g = rwkv_prep(proj, row_off, B, S, lay, sh_misc[:, None, :], rw_prm)
            rkv_t = proj[row_off:row_off + B * S, :3 * W].reshape(B, S, 3, H_rw, HEAD)
            rkv_t = rkv_t.transpose(2, 1, 4, 0, 3).reshape(3, S, HEAD, N)
            prev0 = sh_rkv.reshape(B, 3, H_rw, HEAD).transpose(1, 3, 0, 2).reshape(3, HEAD, N)
            s0 = wkv_prev.transpose(2, 3, 0, 1).reshape(HEAD, HEAD, N)
            a_t, dec_t = _to_lanes(a, B, S, H_rw), _to_lanes(dec, B, S, H_rw)
            n_pad = -N % LANES
            if n_pad:
                padl = lambda x: jnp.pad(x, [(0, 0)] * (x.ndim - 1) + [(0, n_pad)])
                rkv_t, a_t, dec_t, prev0, s0 = map(padl, (rkv_t, a_t, dec_t, prev0, s0))
            o_t, s_t = rwkv_scan(rkv_t[0], rkv_t[1], rkv_t[2], a_t, dec_t, prev0, mu_t, prm_t, s0)
            o_a = rwkv_gate(_from_lanes(o_t[..., :N], B, S, H_rw), g)
            wkv_new = s_t[..., :N].reshape(HEAD, HEAD, B, H_rw).transpose(2, 3, 0, 1)
            last = proj[row_off + jnp.arange(B) * S + (S - 1)]
            mo = lay["misc_off"]
            shift_new = jnp.concatenate(
                [last[:, :3 * W], last[:, mo + n_gate:mo + n_gate + n_dec + n_icl], last[:, mo:mo + n_gate]], axis=-1)
            return o_a, wkv_new, shift_new

        zero_shift = jnp.zeros((Bp, mu_shift.shape[1]), F32)
        zero_wkv = jnp.zeros((Bp, H_rw, HEAD, HEAD), F32)
        oa_p, wkv_p, sh_p = rwkv_group(0, Bp, Sp, zero_shift, zero_wkv)
        oa_s, wkv_s, sh_s = rwkv_group(Tp, Bs, Ss, state_shift[l], state_wkv[l])
        o_a = jnp.concatenate([oa_p, oa_s], axis=0)

        qn, ckv_f, ckv_b, kpe_f, kpe_b = mla_prep(proj, lay, q_norm_g[l], kv_norm_g[l], cos_t, sin_t)
        wq = w_q_b[l]
        wq_perm = jnp.concatenate([wq[:, :, :nope].reshape(q_rank, H_mla * nope),
                                   wq[:, :, nope:].reshape(q_rank, H_mla * rope)], axis=1).astype(BF16)
        qa = matmul(qn, wq_perm, F32, name="q_proj")
        w_uk_t = jnp.transpose(w_uk[l], (1, 2, 0)).astype(BF16)
        w_uv_t = jnp.transpose(w_uv[l], (1, 0, 2)).astype(BF16)
        ql_p, qp_p = q_absorb(qa, 0, Tp, H_mla, nope, rope, kv_rank, w_uk_t, cos_t, sin_t, scale, BF16)
        ql_s, qp_s = q_absorb(qa, Tp, Ts, H_mla, nope, rope, kv_rank, w_uk_t, cos_t, sin_t, scale, F32)
        ob_p = attn_prompt(ql_p, qp_p, ckv_b, kpe_b, w_uv_t, Bp, Sp)
        ob_s = attn_sample(ql_s, qp_s, ckv_f, kpe_f, Tp, w_uv_t, cache_kv_latent, cache_kpe_t,
                           page_table, l, Bs, Ss)
        o_b = jnp.concatenate([ob_p, ob_s], axis=0)

        mixin = gated_merge(o_a, o_b, w_branch[l].astype(BF16), proj, lay["gate_off"], D)
        pre = matmul_residual(mixin, w_out[l].astype(BF16), h, alpha)
        x1, x1b, x1p = layer_norm_rows(pre, ln_mix_g[l], ln_mix_b[l], packed=True)

        idx, wts, rank, counts = route(x1, w_router[l], router_bias[l])
        bm = 384 if (T * TOP_K) // E >= 256 else 128
        blocks_per_e = (counts.astype(I32) + bm - 1) // bm
        base = ((jnp.cumsum(blocks_per_e, axis=1) - blocks_per_e) * bm).astype(F32)
        dest = slot_of_assignment(idx, rank, base)
        slot_tok, blk_expert, n_used = _block_tables(counts, dest, T, TOP_K, E, bm)
        x_sorted = x1p[slot_tok]
        y_sorted = expert_ffn(x_sorted, blk_expert, n_used, w_exp_up, w_exp_down, l, bm)
        routed = combine(y_sorted[dest.T], wts)
        h = shared_ffn_final(x1b, x1, routed, w_shared_up[l].astype(BF16), w_shared_down[l].astype(BF16),
                             ln_ffn_g[l], ln_ffn_b[l], alpha)
        if l + 1 < L:
            hb = h.astype(BF16)

        st_p.append((ckv_f[:Tp].reshape(Bp, Sp, kv_rank), kpe_f[:Tp].reshape(Bp, Sp, rope), wkv_p, sh_p))
        st_s.append((ckv_f[Tp:].reshape(Bs, Ss, kv_rank), kpe_f[Tp:].reshape(Bs, Ss, rope), wkv_s, sh_s))

    y_p = h[:Tp].reshape(Bp, Sp, D)
    y_s = h[Tp:].reshape(Bs, Ss, D)
    stack = lambda sts, k: jnp.stack([s[k] for s in sts])
    return (y_p, y_s, stack(st_p, 0), stack(st_p, 1), stack(st_p, 2), stack(st_p, 3),
            stack(st_s, 0), stack(st_s, 1), stack(st_s, 2), stack(st_s, 3))
```
